```python
import math
import jax, jax.numpy as jnp
from jax import lax
import numpy as np

D_MODEL = 2048
BATCH = 8
SEQ = 4096
DEPTH = 2

ATTN_HEADS = 8
ATTN_QK_DIM = 64
ATTN_V_DIM = 2 * ATTN_QK_DIM
ATTN_WIDTH = ATTN_HEADS * ATTN_V_DIM
Q_BLOCK = 128
SSM_WIDTH = D_MODEL // 2
SSM_GROUP = 16
SSM_GROUPS = SSM_WIDTH // SSM_GROUP
SSM_STATE = 64
DT_MIN = 1e-3
DT_MAX = 1e-1
N_EXPERT_GROUPS = 4
EXPERTS_PER_GROUP = 4
N_EXPERTS = N_EXPERT_GROUPS * EXPERTS_PER_GROUP
TOP_K_INNER = 2
EXPERT_FF = D_MODEL // 2
RMS_EPS = 1e-6

Q_OFF = 0
K_OFF = Q_OFF + ATTN_WIDTH
V_OFF = K_OFF + ATTN_WIDTH
U_OFF = V_OFF + ATTN_WIDTH
GA_OFF = U_OFF + SSM_WIDTH
GB_OFF = GA_OFF + D_MODEL
IN_COLS = GB_OFF + D_MODEL

kernel_name = "hybrid_diffattn_s5_hmoe_encoder"


def rmsnorm(x, g):
    xf = x.astype(jnp.float32)
    y = xf * lax.rsqrt(jnp.mean(xf * xf, axis=-1, keepdims=True) + RMS_EPS)
    return (y * g.astype(jnp.float32)).astype(x.dtype)


def diff_attention(q, k, v, q_gain, k_gain, lam, slopes):
    B, S, H = q.shape[0], q.shape[1], q.shape[2]
    scale = ATTN_QK_DIM ** -0.5
    qf = rmsnorm(q, q_gain).astype(jnp.float32) * scale
    kf = rmsnorm(k, k_gain).astype(jnp.float32)
    vf = v.astype(jnp.float32)
    nb = S // Q_BLOCK
    qb = qf.reshape(B, nb, Q_BLOCK, H, 2, ATTN_QK_DIM).transpose(1, 0, 2, 3, 4, 5)
    kpos = jnp.arange(S, dtype=jnp.float32)

    def block(args):
        qi, i = args
        qpos = (i * Q_BLOCK + jnp.arange(Q_BLOCK)).astype(jnp.float32)
        dist = jnp.abs(qpos[:, None] - kpos[None, :])
        bias = -slopes[:, None, None, None] * dist[None, None]
        s = jnp.einsum('bqhcd,bkhcd->bhcqk', qi, kf) + bias
        p = jax.nn.softmax(s, axis=-1)
        w = p[:, :, 0] - lam * p[:, :, 1]
        return jnp.einsum('bhqk,bkhe->bqhe', w, vf)

    out = lax.map(block, (qb, jnp.arange(nb)))
    return out.transpose(1, 0, 2, 3, 4).reshape(B, S, H, ATTN_V_DIM)


def s5_direction(ug, lam_re, lam_im, log_dt, b_re, b_im, c_re, c_im, reverse):
    S = ug.shape[1]
    dt = jnp.exp(log_dt.astype(jnp.float32))[:, None]
    lr = lam_re.astype(jnp.float32)
    li = lam_im.astype(jnp.float32)
    mag = jnp.exp(lr * dt)
    a_re = mag * jnp.cos(li * dt)
    a_im = mag * jnp.sin(li * dt)
    nr = a_re - 1.0
    den = lr * lr + li * li
    f_re = (nr * lr + a_im * li) / den
    f_im = (a_im * lr - nr * li) / den
    br = b_re.astype(jnp.float32)
    bi = b_im.astype(jnp.float32)
    bb_re = f_re[..., None] * br - f_im[..., None] * bi
    bb_im = f_re[..., None] * bi + f_im[..., None] * br
    bu_re = jnp.einsum('gnp,bsgp->bsgn', bb_re, ug)
    bu_im = jnp.einsum('gnp,bsgp->bsgn', bb_im, ug)
    a_re_s = jnp.broadcast_to(a_re[None, None], (1, S) + a_re.shape)
    a_im_s = jnp.broadcast_to(a_im[None, None], (1, S) + a_im.shape)

    def combine(e1, e2):
        a1r, a1i, b1r, b1i = e1
        a2r, a2i, b2r, b2i = e2
        return (a1r * a2r - a1i * a2i,
                a1r * a2i + a1i * a2r,
                a2r * b1r - a2i * b1i + b2r,
                a2r * b1i + a2i * b1r + b2i)

    _, _, h_re, h_im = lax.associative_scan(
        combine, (a_re_s, a_im_s, bu_re, bu_im), reverse=reverse, axis=1)
    return (jnp.einsum('gpn,bsgn->bsgp', c_re.astype(jnp.float32), h_re)
            - jnp.einsum('gpn,bsgn->bsgp', c_im.astype(jnp.float32), h_im))


def s5_bidirectional(u, lam_re, lam_im, log_dt, b_re, b_im, c_re, c_im, d_skip, w_glu):
    B, S, W = u.shape
    ug = u.astype(jnp.float32).reshape(B, S, SSM_GROUPS, SSM_GROUP)
    y_f = s5_direction(ug, lam_re[0], lam_im[0], log_dt[0], b_re[0], b_im[0],
                       c_re[0], c_im[0], False)
    y_b = s5_direction(ug, lam_re[1], lam_im[1], log_dt[1], b_re[1], b_im[1],
                       c_re[1], c_im[1], True)
    d = d_skip.astype(jnp.float32).reshape(SSM_GROUPS, SSM_GROUP)
    y = (y_f + y_b + d * ug).reshape(B, S, W)
    y = jax.nn.gelu(y).astype(u.dtype)
    return y * jax.nn.sigmoid(y @ w_glu)


def hybrid_mixer(xn, layer_idx, slopes, w_in, q_gain, k_gain, lam_q1, lam_k1, lam_q2,
                 lam_k2, head_gain, w_attn_up, ssm_lam_re, ssm_lam_im, ssm_log_dt,
                 ssm_b_re, ssm_b_im, ssm_c_re, ssm_c_im, ssm_d, w_glu, w_ssm_up, w_out):
    B, S, _ = xn.shape
    h = xn @ w_in
    q = h[..., Q_OFF:K_OFF].reshape(B, S, ATTN_HEADS, 2, ATTN_QK_DIM)
    k = h[..., K_OFF:V_OFF].reshape(B, S, ATTN_HEADS, 2, ATTN_QK_DIM)
    v = h[..., V_OFF:U_OFF].reshape(B, S, ATTN_HEADS, ATTN_V_DIM)
    u = h[..., U_OFF:GA_OFF]
    g_a = h[..., GA_OFF:GB_OFF]
    g_b = h[..., GB_OFF:IN_COLS]

    lam_init = 0.8 - 0.6 * math.exp(-0.3 * layer_idx)
    f32 = jnp.float32
    lam = (jnp.exp(jnp.sum(lam_q1.astype(f32) * lam_k1.astype(f32)))
           - jnp.exp(jnp.sum(lam_q2.astype(f32) * lam_k2.astype(f32))) + lam_init)
    o = diff_attention(q, k, v, q_gain, k_gain, lam, slopes)
    o = rmsnorm(o, head_gain) * (1.0 - lam_init)
    a_out = o.reshape(B, S, ATTN_WIDTH).astype(xn.dtype) @ w_attn_up

    s_out = s5_bidirectional(u, ssm_lam_re, ssm_lam_im, ssm_log_dt, ssm_b_re, ssm_b_im,
                             ssm_c_re, ssm_c_im, ssm_d, w_glu)
    b_out = s_out @ w_ssm_up

    merged = jax.nn.sigmoid(g_a) * a_out + jax.nn.sigmoid(g_b) * b_out
    return merged @ w_out


def hier_moe(xn, w_rg, b_rg, w_re, b_re, w1, w3, w2):
    B, S, D = xn.shape
    t = xn.reshape(B * S, D)
    T = t.shape[0]
    tf = t.astype(jnp.float32)
    g_logits = tf @ w_rg.astype(jnp.float32) + b_rg.astype(jnp.float32)
    g_prob = jax.nn.softmax(g_logits, axis=-1)
    _, g_idx = lax.top_k(g_logits, 1)
    g_w = jnp.take_along_axis(g_prob, g_idx, axis=1)[:, 0]
    e_logits = (tf @ w_re.astype(jnp.float32) + b_re.astype(jnp.float32)).reshape(
        T, N_EXPERT_GROUPS, EXPERTS_PER_GROUP)
    sel = jnp.broadcast_to(g_idx[:, :, None], (T, 1, EXPERTS_PER_GROUP))
    e_in = jnp.take_along_axis(e_logits, sel, axis=1)[:, 0]
    top_v, top_i = lax.top_k(e_in, TOP_K_INNER)
    top_w = jax.nn.softmax(top_v, axis=-1) * g_w[:, None]
    eid = g_idx * EXPERTS_PER_GROUP + top_i
    combine = jnp.sum(jax.nn.one_hot(eid, N_EXPERTS, dtype=jnp.float32)
                      * top_w[..., None], axis=1)
    out = jnp.zeros((T, D), jnp.float32)
    for e in range(N_EXPERTS):
        hid = jax.nn.silu(t @ w1[e]) * (t @ w3[e])
        out = out + combine[:, e:e + 1] * (hid @ w2[e]).astype(jnp.float32)
    return out.reshape(B, S, D).astype(xn.dtype)


def setup_inputs(seed: int = 0) -> dict:
    key = jax.random.key(seed)
    ks = iter(jax.random.split(key, 40))
    L = DEPTH
    f32 = jnp.float32

    def nrm(shape, scale):
        return scale * jax.random.normal(next(ks), shape, f32)

    x = jax.random.normal(next(ks), (BATCH, SEQ, D_MODEL), f32)
    norm1 = 1.0 + nrm((L, D_MODEL), 0.02)
    w_in = nrm((L, D_MODEL, IN_COLS), D_MODEL ** -0.5)
    q_gain = 1.0 + nrm((L, ATTN_QK_DIM), 0.02)
    k_gain = 1.0 + nrm((L, ATTN_QK_DIM), 0.02)
    lam_q1 = nrm((L, ATTN_QK_DIM), 0.1)
    lam_k1 = nrm((L, ATTN_QK_DIM), 0.1)
    lam_q2 = nrm((L, ATTN_QK_DIM), 0.1)
    lam_k2 = nrm((L, ATTN_QK_DIM), 0.1)
    head_gain = 1.0 + nrm((L, ATTN_V_DIM), 0.02)
    w_attn_up = nrm((L, ATTN_WIDTH, D_MODEL), ATTN_WIDTH ** -0.5)
    n_idx = jnp.arange(SSM_STATE, dtype=f32)
    ssm_lam_re = -0.5 + nrm((L, 2, SSM_GROUPS, SSM_STATE), 0.01)
    ssm_lam_im = jnp.pi * n_idx + nrm((L, 2, SSM_GROUPS, SSM_STATE), 0.01)
    ssm_log_dt = jax.random.uniform(next(ks), (L, 2, SSM_GROUPS), f32,
                                    minval=math.log(DT_MIN), maxval=math.log(DT_MAX))
    b_scale = (2.0 * SSM_GROUP) ** -0.5
    c_scale = (2.0 * SSM_STATE) ** -0.5
    ssm_b_re = nrm((L, 2, SSM_GROUPS, SSM_STATE, SSM_GROUP), b_scale)
    ssm_b_im = nrm((L, 2, SSM_GROUPS, SSM_STATE, SSM_GROUP), b_scale)
    ssm_c_re = nrm((L, 2, SSM_GROUPS, SSM_GROUP, SSM_STATE), c_scale)
    ssm_c_im = nrm((L, 2, SSM_GROUPS, SSM_GROUP, SSM_STATE), c_scale)
    ssm_d = nrm((L, SSM_WIDTH), 1.0)
    w_glu = nrm((L, SSM_WIDTH, SSM_WIDTH), SSM_WIDTH ** -0.5)
    w_ssm_up = nrm((L, SSM_WIDTH, D_MODEL), SSM_WIDTH ** -0.5)
    w_out = nrm((L, D_MODEL, D_MODEL), D_MODEL ** -0.5)
    norm2 = 1.0 + nrm((L, D_MODEL), 0.02)
    w_router_group = nrm((L, D_MODEL, N_EXPERT_GROUPS), D_MODEL ** -0.5)
    b_router_group = nrm((L, N_EXPERT_GROUPS), 0.01)
    w_router_expert = nrm((L, D_MODEL, N_EXPERTS), D_MODEL ** -0.5)
    b_router_expert = nrm((L, N_EXPERTS), 0.01)
    w1 = nrm((L, N_EXPERTS, D_MODEL, EXPERT_FF), D_MODEL ** -0.5)
    w3 = nrm((L, N_EXPERTS, D_MODEL, EXPERT_FF), D_MODEL ** -0.5)
    w2 = nrm((L, N_EXPERTS, EXPERT_FF, D_MODEL), EXPERT_FF ** -0.5)
    return {"x": x, "norm1": norm1, "w_in": w_in, "q_gain": q_gain, "k_gain": k_gain,
            "lam_q1": lam_q1, "lam_k1": lam_k1, "lam_q2": lam_q2, "lam_k2": lam_k2,
            "head_gain": head_gain, "w_attn_up": w_attn_up, "ssm_lam_re": ssm_lam_re,
            "ssm_lam_im": ssm_lam_im, "ssm_log_dt": ssm_log_dt, "ssm_b_re": ssm_b_re,
            "ssm_b_im": ssm_b_im, "ssm_c_re": ssm_c_re, "ssm_c_im": ssm_c_im,
            "ssm_d": ssm_d, "w_glu": w_glu, "w_ssm_up": w_ssm_up, "w_out": w_out,
            "norm2": norm2, "w_router_group": w_router_group,
            "b_router_group": b_router_group, "w_router_expert": w_router_expert,
            "b_router_expert": b_router_expert, "w1": w1, "w3": w3, "w2": w2}


def reference(x, norm1, w_in, q_gain, k_gain, lam_q1, lam_k1, lam_q2, lam_k2, head_gain,
              w_attn_up, ssm_lam_re, ssm_lam_im, ssm_log_dt, ssm_b_re, ssm_b_im,
              ssm_c_re, ssm_c_im, ssm_d, w_glu, w_ssm_up, w_out, norm2,
              w_router_group, b_router_group, w_router_expert, b_router_expert,
              w1, w3, w2):
    slopes = jnp.exp2(-8.0 * jnp.arange(1, ATTN_HEADS + 1, dtype=jnp.float32) / ATTN_HEADS)
    for l in range(DEPTH):
        xn = rmsnorm(x, norm1[l])
        x = x + hybrid_mixer(xn, l, slopes, w_in[l], q_gain[l], k_gain[l], lam_q1[l],
                             lam_k1[l], lam_q2[l], lam_k2[l], head_gain[l], w_attn_up[l],
                             ssm_lam_re[l], ssm_lam_im[l], ssm_log_dt[l], ssm_b_re[l],
                             ssm_b_im[l], ssm_c_re[l], ssm_c_im[l], ssm_d[l], w_glu[l],
                             w_ssm_up[l], w_out[l])
        xn2 = rmsnorm(x, norm2[l])
        x = x + hier_moe(xn2, w_router_group[l], b_router_group[l], w_router_expert[l],
                         b_router_expert[l], w1[l], w3[l], w2[l])
    return x
```

```python
import functools
import math

import jax
import jax.numpy as jnp
from jax import lax
from jax.experimental import pallas as pl
from jax.experimental.pallas import tpu as pltpu

F32 = jnp.float32
BF16 = jnp.bfloat16
RMS_EPS = 1e-6
LANES = 128
SUBLANES = 8
VMEM_LIMIT = 56 * 1024 * 1024
S5_CHUNK = 32
N_PAIRS = 6
_PAIR_A = (0, 0, 0, 1, 1, 2)
_PAIR_B = (1, 2, 3, 2, 3, 3)


def _cparams(sem):
    return pltpu.CompilerParams(dimension_semantics=sem, vmem_limit_bytes=VMEM_LIMIT)


def _sigmoid(x):
    return 1.0 / (1.0 + jnp.exp(-x))


def _pick_tile(n, target, unit=LANES):
    if n <= target:
        return n
    t = (target // unit) * unit
    while n % t:
        t -= unit
    return t


def _split_bf16(x):
    hi = x.astype(BF16)
    lo = (x - hi.astype(F32)).astype(BF16)
    return hi, lo


def _norm_inproj_kernel(x_ref, g_ref, w_ref, o_ref, xn_ref):
    @pl.when(pl.program_id(1) == 0)
    def _():
        x = x_ref[...]
        ms = jnp.mean(x * x, axis=-1, keepdims=True)
        xn_ref[...] = (x * lax.rsqrt(ms + RMS_EPS) * g_ref[...]).astype(BF16)

    o_ref[...] = jnp.dot(xn_ref[...], w_ref[...], preferred_element_type=F32).astype(o_ref.dtype)


def _norm_inproj(x2d, gain, w, *, tm=1024, tn=1024):
    T, D = x2d.shape
    N = w.shape[1]
    tm, tn = _pick_tile(T, tm, SUBLANES), _pick_tile(N, tn)
    return pl.pallas_call(
        _norm_inproj_kernel,
        grid=(T // tm, N // tn),
        in_specs=[pl.BlockSpec((tm, D), lambda i, j: (i, 0)),
                  pl.BlockSpec((1, D), lambda i, j: (0, 0)),
                  pl.BlockSpec((D, tn), lambda i, j: (0, j))],
        out_specs=pl.BlockSpec((tm, tn), lambda i, j: (i, j)),
        out_shape=jax.ShapeDtypeStruct((T, N), BF16),
        scratch_shapes=[pltpu.VMEM((tm, D), BF16)],
        compiler_params=_cparams(("parallel", "arbitrary")),
        name="norm_inproj",
    )(x2d, gain.reshape(1, D).astype(F32), w)


def _attn_kernel(scal_ref, slope_ref, q_ref, k_ref, v_ref, qg_ref, kg_ref, hg_ref, o_ref,
                 kn_ref, m_ref, l_ref, acc_ref, *, tq, tk, dk):
    h = pl.program_id(1)
    i = pl.program_id(2)
    S = k_ref.shape[1]
    lane = lax.broadcasted_iota(jnp.int32, (1, 2 * dk), 1)
    first = lane < dk
    r = lax.broadcasted_iota(jnp.int32, (2 * dk, 2 * dk), 0) < dk
    c = lax.broadcasted_iota(jnp.int32, (2 * dk, 2 * dk), 1) < dk
    ones_bd = (r == c).astype(BF16)

    def comp_norm(x, gain):
        sq_hi, sq_lo = _split_bf16(x * x)
        ss = (jnp.dot(sq_hi, ones_bd, preferred_element_type=F32)
              + jnp.dot(sq_lo, ones_bd, preferred_element_type=F32))
        return x * lax.rsqrt(ss * (1.0 / dk) + RMS_EPS) * gain

    @pl.when(i == 0)
    def _():
        kn_ref[...] = comp_norm(k_ref[0].astype(F32), kg_ref[...]).astype(BF16)

    qn = comp_norm(q_ref[0].astype(F32), qg_ref[...]) * (dk ** -0.5)
    q_maps = (jnp.where(first, qn, 0.0).astype(BF16), jnp.where(first, 0.0, qn).astype(BF16))

    m_ref[...] = jnp.full(m_ref.shape, -jnp.inf, F32)
    l_ref[...] = jnp.zeros(l_ref.shape, F32)
    acc_ref[...] = jnp.zeros(acc_ref.shape, F32)
    neg_slope = -slope_ref[h]
    rowpos = jnp.asarray(i * tq, F32) + lax.broadcasted_iota(jnp.int32, (tq, 1), 0).astype(F32)
    colbase = lax.broadcasted_iota(jnp.int32, (1, tk), 1).astype(F32)

    def body(j, carry):
        start = pl.multiple_of(j * tk, tk)
        k = kn_ref[pl.ds(start, tk), :]
        v = v_ref[0, pl.ds(start, tk), :]
        colpos = colbase + jnp.asarray(j * tk, F32)
        bias = neg_slope * jnp.abs(rowpos - colpos)
        for c_idx in range(2):
            s = lax.dot_general(q_maps[c_idx], k, (((1,), (1,)), ((), ())),
                                preferred_element_type=F32) + bias
            m_prev = m_ref[c_idx]
            m_new = jnp.maximum(m_prev, jnp.max(s, axis=-1, keepdims=True))
            alpha = jnp.exp(m_prev - m_new)
            p = jnp.exp(s - m_new)
            l_ref[c_idx] = alpha * l_ref[c_idx] + jnp.sum(p, axis=-1, keepdims=True)
            acc_ref[c_idx] = alpha * acc_ref[c_idx] + jnp.dot(
                p.astype(BF16), v, preferred_element_type=F32)
            m_ref[c_idx] = m_new
        return carry

    lax.fori_loop(0, S // tk, body, 0)

    lam = scal_ref[0]
    out_scale = scal_ref[1]
    o = acc_ref[0] / l_ref[0] - lam * (acc_ref[1] / l_ref[1])
    ms = jnp.mean(o * o, axis=-1, keepdims=True)
    o_ref[0] = (o * lax.rsqrt(ms + RMS_EPS) * hg_ref[...] * out_scale).astype(o_ref.dtype)


def _diff_attention(h3, scal, slopes, q_gain, k_gain, head_gain, *, heads, dk, k_blk, v_blk,
                    tq=256, tk=512):
    B, S, _ = h3.shape
    dv = 2 * dk
    tq, tk = min(tq, S), min(tk, S)
    qg = jnp.tile(q_gain.astype(F32), 2).reshape(1, dv)
    kg = jnp.tile(k_gain.astype(F32), 2).reshape(1, dv)
    hg = head_gain.astype(F32).reshape(1, dv)
    smem = pl.BlockSpec(memory_space=pltpu.SMEM)
    return pl.pallas_call(
        functools.partial(_attn_kernel, tq=tq, tk=tk, dk=dk),
        grid=(B, heads, S // tq),
        in_specs=[smem, smem,
                  pl.BlockSpec((1, tq, dv), lambda b, h, i: (b, i, h)),
                  pl.BlockSpec((1, S, dv), lambda b, h, i: (b, 0, k_blk + h)),
                  pl.BlockSpec((1, S, dv), lambda b, h, i: (b, 0, v_blk + h)),
                  pl.BlockSpec((1, dv), lambda b, h, i: (0, 0)),
                  pl.BlockSpec((1, dv), lambda b, h, i: (0, 0)),
                  pl.BlockSpec((1, dv), lambda b, h, i: (0, 0))],
        out_specs=pl.BlockSpec((1, tq, dv), lambda b, h, i: (b, i, h)),
        out_shape=jax.ShapeDtypeStruct((B, S, heads * dv), BF16),
        scratch_shapes=[pltpu.VMEM((S, dv), BF16),
                        pltpu.VMEM((2, tq, 1), F32),
                        pltpu.VMEM((2, tq, 1), F32),
                        pltpu.VMEM((2, tq, dv), F32)],
        compiler_params=_cparams(("parallel", "parallel", "arbitrary")),
        name="diff_attn",
    )(scal, slopes, h3, h3, h3, qg, kg, hg)


def _complex_powers(a_re, a_im, n):
    p_re = jnp.ones((1,) + a_re.shape, F32)
    p_im = jnp.zeros((1,) + a_re.shape, F32)
    s_re, s_im = a_re, a_im
    while p_re.shape[0] < n + 1:
        q_re = p_re * s_re - p_im * s_im
        q_im = p_re * s_im + p_im * s_re
        p_re = jnp.concatenate([p_re, q_re], axis=0)
        p_im = jnp.concatenate([p_im, q_im], axis=0)
        s_re, s_im = s_re * s_re - s_im * s_im, 2.0 * s_re * s_im
    return p_re[:n + 1], p_im[:n + 1]


def _s5_operators(lam_re, lam_im, log_dt, b_re, b_im, c_re, c_im, d_skip, L):
    hp = lax.Precision.HIGHEST
    G, N, P = b_re.shape[1], b_re.shape[2], b_re.shape[3]
    LP = L * P
    mats = []
    for d in range(2):
        dt = jnp.exp(log_dt[d].astype(F32))[:, None]
        lr, li = lam_re[d].astype(F32), lam_im[d].astype(F32)
        mag = jnp.exp(lr * dt)
        a_re, a_im = mag * jnp.cos(li * dt), mag * jnp.sin(li * dt)
        nr = a_re - 1.0
        den = lr * lr + li * li
        f_re = (nr * lr + a_im * li) / den
        f_im = (a_im * lr - nr * li) / den
        br, bi = b_re[d].astype(F32), b_im[d].astype(F32)
        bb_re = f_re[..., None] * br - f_im[..., None] * bi
        bb_im = f_re[..., None] * bi + f_im[..., None] * br
        pw_re, pw_im = _complex_powers(a_re, a_im, L)
        cr, ci = c_re[d].astype(F32), c_im[d].astype(F32)
        cv_re = cr[None] * pw_re[:, :, None, :] - ci[None] * pw_im[:, :, None, :]
        cv_im = cr[None] * pw_im[:, :, None, :] + ci[None] * pw_re[:, :, None, :]
        kern = (jnp.einsum('tgpn,gnq->tgpq', cv_re[:L], bb_re, precision=hp)
                - jnp.einsum('tgpn,gnq->tgpq', cv_im[:L], bb_im, precision=hp))
        ab_re = pw_re[:L, :, :, None] * bb_re[None] - pw_im[:L, :, :, None] * bb_im[None]
        ab_im = pw_re[:L, :, :, None] * bb_im[None] + pw_im[:L, :, :, None] * bb_re[None]
        mats.append(dict(kern=kern, ab_re=ab_re, ab_im=ab_im, cv_re=cv_re, cv_im=cv_im,
                         aL_re=pw_re[L], aL_im=pw_im[L]))
    f, b = mats
    dsk = d_skip.astype(F32).reshape(G, P)
    k0 = f['kern'][0] + b['kern'][0] + dsk[:, :, None] * jnp.eye(P, dtype=F32)[None]
    kfull = jnp.concatenate([b['kern'][1:][::-1], k0[None], f['kern'][1:]], axis=0)
    idx = (jnp.arange(L)[None, :] - jnp.arange(L)[:, None]) + (L - 1)
    m5 = kfull[idx]
    mt = m5.transpose(2, 0, 4, 1, 3).reshape(G, LP, LP).astype(BF16)

    zeros_sn = jnp.zeros((G, LP, N), F32)

    def st_block(ab, order):
        return ab[order].transpose(1, 0, 3, 2).reshape(G, LP, N)

    rev = jnp.arange(L)[::-1]
    fwd = jnp.arange(L)
    wst = jnp.concatenate([st_block(f['ab_re'], rev), zeros_sn, st_block(f['ab_im'], rev), zeros_sn,
                           st_block(b['ab_re'], fwd), zeros_sn, st_block(b['ab_im'], fwd), zeros_sn],
                          axis=2).astype(BF16)

    zeros_ns = jnp.zeros((G, N, LP), F32)

    def out_block(cv, taus):
        return cv[taus].transpose(1, 3, 0, 2).reshape(G, N, LP)

    tf = jnp.arange(1, L + 1)
    tb = L - jnp.arange(L)
    wout = jnp.concatenate([out_block(f['cv_re'], tf), zeros_ns, -out_block(f['cv_im'], tf), zeros_ns,
                            out_block(b['cv_re'], tb), zeros_ns, -out_block(b['cv_im'], tb), zeros_ns],
                           axis=1).astype(BF16)

    pad = jnp.zeros((G, N), F32)
    rows = [jnp.concatenate([v, pad], axis=1) for v in
            (f['aL_re'], f['aL_im'], b['aL_re'], b['aL_im'])]
    apow = jnp.stack(rows + [jnp.zeros((G, 2 * N), F32)] * 4, axis=1)
    return mt, wst, wout, apow


def _gelu_tanh(y):
    c = math.sqrt(2.0 / math.pi)
    return 0.5 * y * (1.0 + jnp.tanh(c * (y + 0.044715 * (y * y * y))))


def _s5_kernel(ut_ref, mt_ref, wst_ref, wout_ref, ap_ref, y_ref, s_sc, p_sc, *, nc, rows, n2):
    ut = ut_ref[0]
    s_sc[...] = jnp.dot(ut, wst_ref[0], preferred_element_type=F32)
    zero = jnp.zeros((rows, n2), F32)

    def direction(col0, ar, ai, reverse):
        def step(k, carry):
            h_re, h_im = carry
            c = (nc - 1 - k) if reverse else k
            r = pl.ds(pl.multiple_of(c * rows, rows), rows)
            p_sc[r, col0:col0 + n2] = h_re
            p_sc[r, col0 + n2:col0 + 2 * n2] = h_im
            s_re = s_sc[r, col0:col0 + n2]
            s_im = s_sc[r, col0 + n2:col0 + 2 * n2]
            return (ar * h_re - ai * h_im + s_re, ar * h_im + ai * h_re + s_im)
        lax.fori_loop(0, nc, step, (zero, zero))

    direction(0, ap_ref[0, 0:1, :], ap_ref[0, 1:2, :], False)
    direction(2 * n2, ap_ref[0, 2:3, :], ap_ref[0, 3:4, :], True)

    y = (jnp.dot(ut, mt_ref[0], preferred_element_type=F32)
         + jnp.dot(p_sc[...].astype(BF16), wout_ref[0], preferred_element_type=F32))
    y_ref[0] = _gelu_tanh(y).astype(y_ref.dtype)


def _s5_chunked(ut, mt, wst, wout, apow, *, nc, rows):
    G, cols, LP = ut.shape
    n8 = wst.shape[2]
    n2 = n8 // 4
    return pl.pallas_call(
        functools.partial(_s5_kernel, nc=nc, rows=rows, n2=n2),
        grid=(G,),
        in_specs=[pl.BlockSpec((1, cols, LP), lambda g: (g, 0, 0)),
                  pl.BlockSpec((1, LP, LP), lambda g: (g, 0, 0)),
                  pl.BlockSpec((1, LP, n8), lambda g: (g, 0, 0)),
                  pl.BlockSpec((1, n8, LP), lambda g: (g, 0, 0)),
                  pl.BlockSpec((1, SUBLANES, n2), lambda g: (g, 0, 0))],
        out_specs=pl.BlockSpec((1, cols, LP), lambda g: (g, 0, 0)),
        out_shape=jax.ShapeDtypeStruct((G, cols, LP), BF16),
        scratch_shapes=[pltpu.VMEM((cols, n8), F32), pltpu.VMEM((cols, n8), F32)],
        compiler_params=_cparams(("parallel",)),
        name="s5_chunked",
    )(ut, mt, wst, wout, apow)


def _glu_merge_kernel(y_ref, o_ref, ga_ref, gb_ref, wg_ref, wa_ref, wb_ref, out_ref):
    y = y_ref[...]
    z = jnp.dot(y, wg_ref[...], preferred_element_type=F32)
    s = (y.astype(F32) * _sigmoid(z)).astype(BF16)
    a_out = jnp.dot(o_ref[...], wa_ref[...], preferred_element_type=F32)
    b_out = jnp.dot(s, wb_ref[...], preferred_element_type=F32)
    merged = (_sigmoid(ga_ref[...].astype(F32)) * a_out + _sigmoid(gb_ref[...].astype(F32)) * b_out)
    out_ref[...] = merged.astype(out_ref.dtype)


def _glu_merge(y, o, h, wg, wa, wb, *, ga_blk, gb_blk, tm=256):
    T, SW = y.shape
    AW = o.shape[1]
    D = wa.shape[1]
    tm = min(tm, T)
    return pl.pallas_call(
        _glu_merge_kernel,
        grid=(T // tm,),
        in_specs=[pl.BlockSpec((tm, SW), lambda i: (i, 0)),
                  pl.BlockSpec((tm, AW), lambda i: (i, 0)),
                  pl.BlockSpec((tm, D), lambda i: (i, ga_blk)),
                  pl.BlockSpec((tm, D), lambda i: (i, gb_blk)),
                  pl.BlockSpec((SW, SW), lambda i: (0, 0)),
                  pl.BlockSpec((AW, D), lambda i: (0, 0)),
                  pl.BlockSpec((SW, D), lambda i: (0, 0))],
        out_specs=pl.BlockSpec((tm, D), lambda i: (i, 0)),
        out_shape=jax.ShapeDtypeStruct((T, D), BF16),
        compiler_params=_cparams(("parallel",)),
        name="glu_merge",
    )(y, o, h, h, wg, wa, wb)


def _outproj_kernel(m_ref, x_ref, w_ref, o_ref):
    o_ref[...] = x_ref[...] + jnp.dot(m_ref[...], w_ref[...], preferred_element_type=F32)


def _outproj(merged, x2d, w, *, tm=512):
    T, D = x2d.shape
    tm = min(tm, T)
    return pl.pallas_call(
        _outproj_kernel,
        grid=(T // tm,),
        in_specs=[pl.BlockSpec((tm, D), lambda i: (i, 0)),
                  pl.BlockSpec((tm, D), lambda i: (i, 0)),
                  pl.BlockSpec((D, D), lambda i: (0, 0))],
        out_specs=pl.BlockSpec((tm, D), lambda i: (i, 0)),
        out_shape=jax.ShapeDtypeStruct((T, D), F32),
        compiler_params=_cparams(("parallel",)),
        name="outproj",
    )(merged, x2d, w)


def _router_kernel(x_ref, g_ref, whi_ref, wlo_ref, b_ref, xe_ref, meta_ref, cnt_ref,
                   tri_sc, eye_sc, carry_sc, *, n_groups, per_group, e_row0):
    t = pl.program_id(0)
    tm, D = x_ref.shape
    n_rows = whi_ref.shape[0]

    @pl.when(t == 0)
    def _():
        r = lax.broadcasted_iota(jnp.int32, (tm, tm), 0)
        c = lax.broadcasted_iota(jnp.int32, (tm, tm), 1)
        tri_sc[...] = (r < c).astype(BF16)
        eye_sc[...] = (r == c).astype(BF16)
        carry_sc[...] = jnp.zeros(carry_sc.shape, F32)

    x = x_ref[...]
    ms = jnp.mean(x * x, axis=-1, keepdims=True)
    xn = x * lax.rsqrt(ms + RMS_EPS) * g_ref[...]
    xe_ref[:, 0:D] = xn

    nt = (((1,), (1,)), ((), ()))
    x_hi, x_lo = _split_bf16(xn)
    whi, wlo = whi_ref[...], wlo_ref[...]
    logits = (lax.dot_general(whi, x_hi, nt, preferred_element_type=F32)
              + lax.dot_general(whi, x_lo, nt, preferred_element_type=F32)
              + lax.dot_general(wlo, x_hi, nt, preferred_element_type=F32)) + b_ref[...]

    g = [logits[k:k + 1, :] for k in range(n_groups)]
    g_max = functools.reduce(jnp.maximum, g)
    g_idx = jnp.full(g_max.shape, n_groups - 1, jnp.int32)
    for k in range(n_groups - 2, -1, -1):
        g_idx = jnp.where(g[k] == g_max, k, g_idx)
    g_w = 1.0 / functools.reduce(lambda a, b: a + b, [jnp.exp(v - g_max) for v in g])

    e_in = []
    for j in range(per_group):
        v = logits[e_row0 + j:e_row0 + j + 1, :]
        for k in range(1, n_groups):
            row = e_row0 + k * per_group + j
            v = jnp.where(g_idx == k, logits[row:row + 1, :], v)
        e_in.append(v)

    def first_argmax(vals):
        vmax = functools.reduce(jnp.maximum, vals)
        idx = jnp.full(vmax.shape, len(vals) - 1, jnp.int32)
        for k in range(len(vals) - 2, -1, -1):
            idx = jnp.where(vals[k] == vmax, k, idx)
        return vmax, idx

    v1, i1 = first_argmax(e_in)
    rest = [jnp.where(i1 == j, -jnp.inf, e_in[j]) for j in range(per_group)]
    v2, i2 = first_argmax(rest)
    e2 = jnp.exp(v2 - v1)
    w1 = g_w / (1.0 + e2)
    w2 = g_w * e2 / (1.0 + e2)
    lo_first = i1 < i2
    a = jnp.where(lo_first, i1, i2)
    b = jnp.where(lo_first, i2, i1)
    w_a = jnp.where(lo_first, w1, w2)
    w_b = jnp.where(lo_first, w2, w1)
    pair = jnp.where(a == 0, b - 1, jnp.where(a == 1, b + 1, 5))
    bucket = g_idx * N_PAIRS + pair

    rows = lax.broadcasted_iota(jnp.int32, (n_rows, tm), 0)
    onehot = (rows == bucket).astype(F32)
    prefix = jnp.dot(onehot.astype(BF16), tri_sc[...], preferred_element_type=F32)
    rank = jnp.sum(onehot * (prefix + carry_sc[...]), axis=0, keepdims=True)
    carry_sc[...] = carry_sc[...] + jnp.sum(onehot, axis=1, keepdims=True)
    cnt_ref[...] = jnp.broadcast_to(carry_sc[...], cnt_ref.shape)

    meta_ref[0, 0:1, :] = bucket.astype(F32)
    meta_ref[0, 1:2, :] = rank
    meta_ref[0, 2:3, :] = w_a
    meta_ref[0, 3:4, :] = w_b
    meta_ref[0, 4:8, :] = jnp.zeros((4, tm), F32)

    eye = eye_sc[...]
    for blk, w in enumerate((w_a, w_b)):
        wrep = jnp.broadcast_to(w, (LANES, tm))
        hi, lo = _split_bf16(wrep)
        col = (lax.dot_general(eye, hi, nt, preferred_element_type=F32)
               + lax.dot_general(eye, lo, nt, preferred_element_type=F32))
        xe_ref[:, D + blk * LANES:D + (blk + 1) * LANES] = col


def _router(x2d, gain, w_rg, b_rg, w_re, b_re, *, tm=512):
    T, D = x2d.shape
    n_groups = w_rg.shape[1]
    n_exp = w_re.shape[1]
    per_group = n_exp // n_groups
    e_row0 = SUBLANES
    n_rows = e_row0 + n_exp
    n_rows = ((n_rows + SUBLANES - 1) // SUBLANES) * SUBLANES
    assert n_groups * N_PAIRS <= n_rows and per_group == 4
    wt = jnp.zeros((n_rows, D), F32)
    wt = wt.at[0:n_groups].set(w_rg.astype(F32).T).at[e_row0:e_row0 + n_exp].set(w_re.astype(F32).T)
    bias = jnp.zeros((n_rows, 1), F32)
    bias = bias.at[0:n_groups, 0].set(b_rg.astype(F32)).at[e_row0:e_row0 + n_exp, 0].set(b_re.astype(F32))
    w_hi, w_lo = _split_bf16(wt)
    tm = min(tm, T)
    De = D + 2 * LANES
    return pl.pallas_call(
        functools.partial(_router_kernel, n_groups=n_groups, per_group=per_group, e_row0=e_row0),
        grid=(T // tm,),
        in_specs=[pl.BlockSpec((tm, D), lambda t: (t, 0)),
                  pl.BlockSpec((1, D), lambda t: (0, 0)),
                  pl.BlockSpec((n_rows, D), lambda t: (0, 0)),
                  pl.BlockSpec((n_rows, D), lambda t: (0, 0)),
                  pl.BlockSpec((n_rows, 1), lambda t: (0, 0))],
        out_specs=[pl.BlockSpec((tm, De), lambda t: (t, 0)),
                   pl.BlockSpec((1, SUBLANES, tm), lambda t: (t, 0, 0)),
                   pl.BlockSpec((n_rows, LANES), lambda t: (0, 0))],
        out_shape=[jax.ShapeDtypeStruct((T, De), F32),
                   jax.ShapeDtypeStruct((T // tm, SUBLANES, tm), F32),
                   jax.ShapeDtypeStruct((n_rows, LANES), F32)],
        scratch_shapes=[pltpu.VMEM((tm, tm), BF16), pltpu.VMEM((tm, tm), BF16),
                        pltpu.VMEM((n_rows, 1), F32)],
        compiler_params=_cparams(("arbitrary",)),
        name="router",
    )(x2d, gain.reshape(1, D).astype(F32), w_hi, w_lo, bias)


def _dispatch_kernel(pos_ref, x_ref, xs_in_ref, xs_ref, sem, *, tm):
    del xs_in_ref
    base = pl.program_id(0) * tm

    def issue(r, carry):
        pltpu.make_async_copy(x_ref.at[pl.ds(r, 1)], xs_ref.at[pl.ds(pos_ref[base + r], 1)], sem).start()
        return carry

    lax.fori_loop(0, tm, issue, 0, unroll=8)

    def drain(r, carry):
        pltpu.make_async_copy(x_ref.at[pl.ds(0, 1)], xs_ref.at[pl.ds(0, 1)], sem).wait()
        return carry

    lax.fori_loop(0, tm, drain, 0, unroll=8)


def _dispatch(pos, xe, n_rows, *, tm=512):
    T, De = xe.shape
    tm = min(tm, T)
    xs0 = jnp.zeros((n_rows, De), F32)
    return pl.pallas_call(
        functools.partial(_dispatch_kernel, tm=tm),
        grid_spec=pltpu.PrefetchScalarGridSpec(
            num_scalar_prefetch=1,
            grid=(T // tm,),
            in_specs=[pl.BlockSpec((tm, De), lambda t, pos: (t, 0)),
                      pl.BlockSpec(memory_space=pl.ANY)],
            out_specs=pl.BlockSpec(memory_space=pl.ANY),
            scratch_shapes=[pltpu.SemaphoreType.DMA(())]),
        out_shape=jax.ShapeDtypeStruct((n_rows, De), F32),
        input_output_aliases={2: 0},
        compiler_params=_cparams(("arbitrary",)),
        name="moe_dispatch",
    )(pos, xe, xs0)


def _collect_kernel(pos_ref, ys_ref, x_ref, o_ref, buf, sem, *, tm):
    base = pl.program_id(0) * tm

    def issue(r, carry):
        pltpu.make_async_copy(ys_ref.at[pl.ds(pos_ref[base + r], 1)], buf.at[pl.ds(r, 1)], sem).start()
        return carry

    lax.fori_loop(0, tm, issue, 0, unroll=8)

    def drain(r, carry):
        pltpu.make_async_copy(ys_ref.at[pl.ds(0, 1)], buf.at[pl.ds(0, 1)], sem).wait()
        return carry

    lax.fori_loop(0, tm, drain, 0, unroll=8)
    o_ref[...] = x_ref[...] + buf[...]


def _collect(pos, ys, x2d, *, tm=512):
    T, D = x2d.shape
    tm = min(tm, T)
    return pl.pallas_call(
        functools.partial(_collect_kernel, tm=tm),
        grid_spec=pltpu.PrefetchScalarGridSpec(
            num_scalar_prefetch=1,
            grid=(T // tm,),
            in_specs=[pl.BlockSpec(memory_space=pl.ANY),
                      pl.BlockSpec((tm, D), lambda t, pos: (t, 0))],
            out_specs=pl.BlockSpec((tm, D), lambda t, pos: (t, 0)),
            scratch_shapes=[pltpu.VMEM((tm, D), F32), pltpu.SemaphoreType.DMA(())]),
        out_shape=jax.ShapeDtypeStruct((T, D), F32),
        compiler_params=_cparams(("arbitrary",)),
        name="moe_collect",
    )(pos, ys, x2d)


def _ffn_kernel(te_ref, tv_ref, x_ref, w1_ref, w3_ref, w2_ref, o_ref, xb_sc, *, D):
    i, j, f = pl.program_id(0), pl.program_id(1), pl.program_id(2)
    first = jnp.logical_and(j == 0, f == 0)

    @pl.when(first)
    def _():
        xb_sc[...] = x_ref[:, 0:D].astype(BF16)
        o_ref[...] = jnp.zeros(o_ref.shape, F32)

    @pl.when(tv_ref[i] != 0)
    def _():
        xb = xb_sc[...]
        h1 = jnp.dot(xb, w1_ref[0], preferred_element_type=F32)
        h3 = jnp.dot(xb, w3_ref[0], preferred_element_type=F32)
        hid = (h1 * _sigmoid(h1) * h3).astype(BF16)
        y = jnp.dot(hid, w2_ref[0], preferred_element_type=F32)
        w = jnp.where(j == 0, x_ref[:, D:D + 1], x_ref[:, D + LANES:D + LANES + 1])
        o_ref[...] += w * y


def _expert_ffn(tile_expert, tile_valid, xs, w1, w3, w2, *, tm, tf=512):
    R, De = xs.shape
    E, D, FF = w1.shape
    tf = min(tf, FF)
    n_tiles = R // tm
    return pl.pallas_call(
        functools.partial(_ffn_kernel, D=D),
        grid_spec=pltpu.PrefetchScalarGridSpec(
            num_scalar_prefetch=2,
            grid=(n_tiles, 2, FF // tf),
            in_specs=[pl.BlockSpec((tm, De), lambda i, j, f, te, tv: (i, 0)),
                      pl.BlockSpec((1, D, tf), lambda i, j, f, te, tv: (te[2 * i + j], 0, f)),
                      pl.BlockSpec((1, D, tf), lambda i, j, f, te, tv: (te[2 * i + j], 0, f)),
                      pl.BlockSpec((1, tf, D), lambda i, j, f, te, tv: (te[2 * i + j], f, 0))],
            out_specs=pl.BlockSpec((tm, D), lambda i, j, f, te, tv: (i, 0)),
            scratch_shapes=[pltpu.VMEM((tm, D), BF16)]),
        out_shape=jax.ShapeDtypeStruct((R, D), F32),
        compiler_params=_cparams(("parallel", "arbitrary", "arbitrary")),
        name="expert_ffn",
    )(tile_expert, tile_valid, xs, w1, w3, w2)


def _moe_plan(meta, counts, n_groups, per_group, T, tm_e):
    n_buckets = n_groups * N_PAIRS
    bucket = meta[:, 0, :].reshape(T).astype(jnp.int32)
    rank = meta[:, 1, :].reshape(T).astype(jnp.int32)
    cnt = counts[:n_buckets, 0].astype(jnp.int32)
    padded = ((cnt + tm_e - 1) // tm_e) * tm_e
    ends = jnp.cumsum(padded)
    offs = ends - padded
    pos = offs[bucket] + rank
    n_tiles = T // tm_e + n_buckets
    starts = jnp.arange(n_tiles, dtype=jnp.int32) * tm_e
    tile_valid = (starts < ends[-1]).astype(jnp.int32)
    last_start = jnp.maximum(ends[-1] - tm_e, 0)
    tb = jnp.searchsorted(ends, jnp.minimum(starts, last_start), side='right').astype(jnp.int32)
    tb = jnp.minimum(tb, n_buckets - 1)
    grp, pair = tb // N_PAIRS, tb % N_PAIRS
    ea = grp * per_group + jnp.asarray(_PAIR_A, jnp.int32)[pair]
    eb = grp * per_group + jnp.asarray(_PAIR_B, jnp.int32)[pair]
    tile_expert = jnp.stack([ea, eb], axis=1).reshape(-1)
    return pos, tile_expert, tile_valid, n_tiles


def _hier_moe_residual(x2d, gain, w_rg, b_rg, w_re, b_re, w1, w3, w2, *, tm_e=512):
    T, D = x2d.shape
    n_groups = w_rg.shape[1]
    per_group = w_re.shape[1] // n_groups
    tm_e = min(tm_e, T)
    xe, meta, counts = _router(x2d, gain, w_rg, b_rg, w_re, b_re)
    pos, tile_expert, tile_valid, n_tiles = _moe_plan(meta, counts, n_groups, per_group, T, tm_e)
    xs = _dispatch(pos, xe, n_tiles * tm_e)
    ys = _expert_ffn(tile_expert, tile_valid, xs, w1, w3, w2, tm=tm_e)
    return _collect(pos, ys, x2d)


def _hybrid_mixer_residual(x2d, B, S, layer_idx, slopes, norm1, w_in, q_gain, k_gain, lam_q1, lam_k1,
                           lam_q2, lam_k2, head_gain, w_attn_up, lam_re, lam_im, log_dt, b_re, b_im,
                           c_re, c_im, d_skip, w_glu, w_ssm_up, w_out):
    T, D = x2d.shape
    dk = q_gain.shape[0]
    dv = head_gain.shape[0]
    AW, SW = w_attn_up.shape[0], w_glu.shape[0]
    heads = AW // dv
    assert dv == 2 * dk == LANES
    u_off = 3 * AW
    ga_off = u_off + SW
    gb_off = ga_off + D
    assert ga_off % D == 0

    h = _norm_inproj(x2d, norm1, w_in.astype(BF16))

    lam_init = 0.8 - 0.6 * math.exp(-0.3 * layer_idx)
    lam = (jnp.exp(jnp.sum(lam_q1.astype(F32) * lam_k1.astype(F32)))
           - jnp.exp(jnp.sum(lam_q2.astype(F32) * lam_k2.astype(F32))) + lam_init)
    scal = jnp.stack([lam, jnp.asarray(1.0 - lam_init, F32)]).astype(F32)
    o = _diff_attention(h.reshape(B, S, -1), scal, slopes, q_gain, k_gain, head_gain,
                        heads=heads, dk=dk, k_blk=AW // dv, v_blk=2 * AW // dv)
    o = o.reshape(T, AW)

    G, P = b_re.shape[1], b_re.shape[3]
    L = min(S5_CHUNK, S)
    nc = S // L
    assert B % SUBLANES == 0
    mt, wst, wout, apow = _s5_operators(lam_re, lam_im, log_dt, b_re, b_im, c_re, c_im, d_skip, L)
    u = h[:, u_off:u_off + SW].reshape(B, nc, L, G, P)
    ut = u.transpose(3, 1, 0, 2, 4).reshape(G, nc * B, L * P)
    yt = _s5_chunked(ut, mt, wst, wout, apow, nc=nc, rows=B)
    y = yt.reshape(G, nc, B, L, P).transpose(2, 1, 3, 0, 4).reshape(T, SW)

    merged = _glu_merge(y, o, h, w_glu.astype(BF16), w_attn_up.astype(BF16), w_ssm_up.astype(BF16),
                        ga_blk=ga_off // D, gb_blk=gb_off // D)
    return _outproj(merged, x2d, w_out.astype(BF16))


def kernel(x, norm1, w_in, q_gain, k_gain, lam_q1, lam_k1, lam_q2, lam_k2, head_gain, w_attn_up, ssm_lam_re, ssm_lam_im, ssm_log_dt, ssm_b_re, ssm_b_im, ssm_c_re, ssm_c_im, ssm_d, w_glu, w_ssm_up, w_out, norm2, w_router_group, b_router_group, w_router_expert, b_router_expert, w1, w3, w2):
    B, S, D = x.shape
    depth = norm1.shape[0]
    heads = w_attn_up.shape[1] // head_gain.shape[1]
    slopes = jnp.exp2(-8.0 * jnp.arange(1, heads + 1, dtype=F32) / heads)
    x2d = x.reshape(B * S, D).astype(F32)
    for l in range(depth):
        x2d = _hybrid_mixer_residual(
            x2d, B, S, l, slopes, norm1[l], w_in[l], q_gain[l], k_gain[l], lam_q1[l], lam_k1[l],
            lam_q2[l], lam_k2[l], head_gain[l], w_attn_up[l], ssm_lam_re[l], ssm_lam_im[l],
            ssm_log_dt[l], ssm_b_re[l], ssm_b_im[l], ssm_c_re[l], ssm_c_im[l], ssm_d[l], w_glu[l],
            w_ssm_up[l], w_out[l])
        x2d = _hier_moe_residual(
            x2d, norm2[l], w_router_group[l], b_router_group[l], w_router_expert[l],
            b_router_expert[l], w1[l].astype(BF16), w3[l].astype(BF16), w2[l].astype(BF16))
    return x2d.reshape(B, S, D).astype(x.dtype)
```

```python
import functools
import math

import jax
import jax.numpy as jnp
from jax import lax
from jax.experimental import pallas as pl
from jax.experimental.pallas import tpu as pltpu

F32 = jnp.float32
BF16 = jnp.bfloat16
RMS_EPS = 1e-6
LOG2E = math.log2(math.e)
LANES = 128
SUBLANES = 8
VMEM_LIMIT = 56 * 1024 * 1024
S5_CHUNK = 32
N_PAIRS = 6
_PAIR_A = (0, 0, 0, 1, 1, 2)
_PAIR_B = (1, 2, 3, 2, 3, 3)


def _cparams(sem):
    return pltpu.CompilerParams(dimension_semantics=sem, vmem_limit_bytes=VMEM_LIMIT)


def _sigmoid(x):
    return 1.0 / (1.0 + jnp.exp(-x))


def _pick_tile(n, target, unit=LANES):
    if n <= target:
        return n
    t = (target // unit) * unit
    while n % t:
        t -= unit
    return t


def _split_bf16(x):
    hi = x.astype(BF16)
    lo = (x - hi.astype(F32)).astype(BF16)
    return hi, lo


def _norm_inproj_kernel(x_ref, g_ref, w_ref, o_ref, xn_ref):
    @pl.when(pl.program_id(1) == 0)
    def _():
        x = x_ref[...]
        ms = jnp.mean(x * x, axis=-1, keepdims=True)
        xn_ref[...] = (x * lax.rsqrt(ms + RMS_EPS) * g_ref[...]).astype(BF16)

    o_ref[...] = jnp.dot(xn_ref[...], w_ref[...], preferred_element_type=F32).astype(o_ref.dtype)


def _norm_inproj(x2d, gain, w, *, tm=1024, tn=1024):
    T, D = x2d.shape
    N = w.shape[1]
    tm, tn = _pick_tile(T, tm, SUBLANES), _pick_tile(N, tn)
    return pl.pallas_call(
        _norm_inproj_kernel,
        grid=(T // tm, N // tn),
        in_specs=[pl.BlockSpec((tm, D), lambda i, j: (i, 0)),
                  pl.BlockSpec((1, D), lambda i, j: (0, 0)),
                  pl.BlockSpec((D, tn), lambda i, j: (0, j))],
        out_specs=pl.BlockSpec((tm, tn), lambda i, j: (i, j)),
        out_shape=jax.ShapeDtypeStruct((T, N), BF16),
        scratch_shapes=[pltpu.VMEM((tm, D), BF16)],
        compiler_params=_cparams(("parallel", "arbitrary")),
        name="norm_inproj",
    )(x2d, gain.reshape(1, D).astype(F32), w)


N_BIAS_LANES = 6


def _attn_kernel(scal_ref, slope_ref, q_ref, k_ref, v_ref, qg_ref, kg_ref, hg_ref, o_ref,
                 ka_ref, kb_ref, vt_ref, *tile_scratch, tq, tk, dk, n_sub):
    k_ref_map = (ka_ref, kb_ref)
    h = pl.program_id(1)
    i = pl.program_id(2)
    S = k_ref.shape[1]
    dv = 2 * dk
    nk = S // tk
    nt = (((1,), (1,)), ((), ()))
    lane = lax.broadcasted_iota(jnp.int32, (1, dv), 1)
    first = lane < dk
    feat_lane = jnp.where(first, lane, lane - dk)
    r = lax.broadcasted_iota(jnp.int32, (dv, dv), 0)
    c = lax.broadcasted_iota(jnp.int32, (dv, dv), 1)
    ones_bd = ((r < dk) == (c < dk)).astype(BF16)
    eye_dv = (r == c).astype(BF16)

    def comp_norm(x, gain):
        sq_hi, sq_lo = _split_bf16(x * x)
        ss = (jnp.dot(sq_hi, ones_bd, preferred_element_type=F32)
              + jnp.dot(sq_lo, ones_bd, preferred_element_type=F32))
        return x * lax.rsqrt(ss * (1.0 / dk) + RMS_EPS) * gain

    @pl.when(i == 0)
    def _():
        kn = comp_norm(k_ref[0].astype(F32), kg_ref[...])
        ds = lax.broadcasted_iota(jnp.int32, (S, dv), 0) & (tk - 1)
        ds_lo = ds & 255
        fl = jnp.broadcast_to(feat_lane, (S, dv))
        feat = jnp.where(fl < 3, ds_lo, jnp.where(fl < N_BIAS_LANES, ds - ds_lo, 0)).astype(F32)
        ka_ref[...] = jnp.where(first, kn, feat).astype(BF16)
        kb_ref[...] = jnp.where(first, feat, kn).astype(BF16)
        for jc in range(nk):
            vt_ref[jc, 0:dv, :] = lax.dot_general(
                eye_dv, v_ref[0, jc * tk:(jc + 1) * tk, :], nt,
                preferred_element_type=F32).astype(BF16)
            vt_ref[jc, dv:dv + 2 * SUBLANES, :] = jnp.ones((2 * SUBLANES, tk), BF16)

    slope2 = slope_ref[h] * LOG2E
    s_full = jnp.full((1, dv), slope2, F32)
    s_hi = s_full.astype(BF16).astype(F32)
    s_mid = (s_full - s_hi).astype(BF16).astype(F32)
    s_lo = (s_full - s_hi - s_mid).astype(BF16).astype(F32)
    q_feat = jnp.where((feat_lane == 0) | (feat_lane == 3), s_hi,
                       jnp.where((feat_lane == 1) | (feat_lane == 4), s_mid,
                                 jnp.where((feat_lane == 2) | (feat_lane == 5), s_lo, 0.0)))

    lam = scal_ref[0]
    out_scale = scal_ref[1]
    rq = lax.broadcasted_iota(jnp.int32, (tq, tq), 0)
    cq = lax.broadcasted_iota(jnp.int32, (tq, tq), 1)
    eye_q = (rq == cq).astype(BF16)
    hg_col = jnp.sum(jnp.where(r == c, jnp.broadcast_to(hg_ref[...], (dv, dv)), 0.0),
                     axis=1, keepdims=True) * out_scale

    m_sets = (tile_scratch[0:2], tile_scratch[2:4])
    acc_sets = (tile_scratch[4:6], tile_scratch[6:8])
    z_slots = (tile_scratch[8:10], tile_scratch[10:12])

    class Tile:
        def __init__(self, sub):
            self.sub = sub
            self.q0 = (i * n_sub + sub) * tq
            self.rows = pl.ds(sub * tq, tq)
            self.m_ref, self.acc_ref = m_sets[sub % 2], acc_sets[sub % 2]
            qn = comp_norm(q_ref[0, self.rows, :].astype(F32), qg_ref[...]) * (LOG2E * dk ** -0.5)
            self.q_plain = (jnp.where(first, qn, 0.0).astype(BF16),
                            jnp.where(first, 0.0, qn).astype(BF16))
            self.q_left = (jnp.where(first, qn, q_feat).astype(BF16),
                           jnp.where(first, q_feat, qn).astype(BF16))
            self.q_right = (jnp.where(first, qn, -q_feat).astype(BF16),
                            jnp.where(first, -q_feat, qn).astype(BF16))
            self.jd = self.q0 // tk
            self.t_row = (jnp.asarray(self.q0, F32)
                          + lax.broadcasted_iota(jnp.int32, (1, tq), 1).astype(F32))

        def chunk_of(self, step):
            if step == 0:
                return self.jd, None
            j = (step - 1) + jnp.where((step - 1) >= self.jd, 1, 0)
            return j, j < self.jd

        def scores(self, step, z_ref):
            j, left = self.chunk_of(step)
            start = pl.multiple_of(j * tk, tk)
            if left is None:
                key = lax.broadcasted_iota(jnp.int32, (tk, tq), 0) + j * tk
                qry = lax.broadcasted_iota(jnp.int32, (tk, tq), 1) + self.q0
                bias = (-slope2) * jnp.abs((qry - key).astype(F32))
            for c_idx in range(2):
                kc = k_ref_map[c_idx][pl.ds(start, tk), :]
                if left is None:
                    z = lax.dot_general(kc, self.q_plain[c_idx], nt,
                                        preferred_element_type=F32) + bias
                else:
                    qv = jnp.where(left, self.q_left[c_idx], self.q_right[c_idx])
                    z = lax.dot_general(kc, qv, nt, preferred_element_type=F32)
                z_ref[c_idx][...] = z

        def softmax_pv(self, step, z_ref):
            j, left = self.chunk_of(step)
            vt = vt_ref[j]
            for c_idx in range(2):
                z = z_ref[c_idx][...]
                if left is None:
                    m_new = jnp.max(z, axis=0, keepdims=True)
                    p = jnp.exp2(z - m_new)
                    self.acc_ref[c_idx][...] = jnp.dot(vt, p.astype(BF16),
                                                       preferred_element_type=F32)
                else:
                    rterm = (jnp.where(left, -slope2, slope2)
                             * (self.t_row - jnp.asarray(j * tk, F32)))
                    m_prev = self.m_ref[c_idx][...]
                    m_new = jnp.maximum(m_prev, jnp.max(z, axis=0, keepdims=True) + rterm)
                    alpha = jnp.exp2(m_prev - m_new)
                    p = jnp.exp2(z - (m_new - rterm))
                    self.acc_ref[c_idx][...] = alpha * self.acc_ref[c_idx][...] + jnp.dot(
                        vt, p.astype(BF16), preferred_element_type=F32)
                self.m_ref[c_idx][...] = m_new

        def finish(self):
            acc = self.acc_ref
            o_t = (acc[0][0:dv, :] / acc[0][dv:dv + 1, :]
                   - lam * (acc[1][0:dv, :] / acc[1][dv:dv + 1, :]))
            ms = jnp.mean(o_t * o_t, axis=0, keepdims=True)
            on_t = (o_t * lax.rsqrt(ms + RMS_EPS) * hg_col).astype(BF16)
            o_ref[0, self.rows, :] = lax.dot_general(
                eye_q, on_t, nt, preferred_element_type=F32).astype(o_ref.dtype)

    items = [(sub, step) for sub in range(n_sub) for step in range(nk)]
    tiles = {0: Tile(0)}
    tiles[0].scores(0, z_slots[0])
    pending_finish = None
    for n, (sub, step) in enumerate(items):
        if n + 2 < len(items) and items[n + 2][1] == 0:
            tiles[items[n + 2][0]] = Tile(items[n + 2][0])
        if n + 1 < len(items):
            sub_n, step_n = items[n + 1]
            if sub_n not in tiles:
                tiles[sub_n] = Tile(sub_n)
            tiles[sub_n].scores(step_n, z_slots[(n + 1) % 2])
        tiles[sub].softmax_pv(step, z_slots[n % 2])
        if pending_finish is not None:
            pending_finish.finish()
            pending_finish = None
        if step == nk - 1:
            pending_finish = tiles[sub]
    pending_finish.finish()


def _diff_attention(h3, scal, slopes, q_gain, k_gain, head_gain, *, heads, dk, k_blk, v_blk,
                    tq=256, tk=512, n_sub=4):
    B, S, _ = h3.shape
    dv = 2 * dk
    tq, tk = min(tq, S), min(tk, S)
    n_sub = min(n_sub, S // tq)
    assert tk % tq == 0 and S % tk == 0
    assert S % (n_sub * tq) == 0
    qg = jnp.tile(q_gain.astype(F32), 2).reshape(1, dv)
    kg = jnp.tile(k_gain.astype(F32), 2).reshape(1, dv)
    hg = head_gain.astype(F32).reshape(1, dv)
    smem = pl.BlockSpec(memory_space=pltpu.SMEM)
    rows_ext = dv + 2 * SUBLANES
    tile_scratch = ([pltpu.VMEM((1, tq), F32)] * 4
                    + [pltpu.VMEM((rows_ext, tq), F32)] * 4
                    + [pltpu.VMEM((tk, tq), F32)] * 4)
    return pl.pallas_call(
        functools.partial(_attn_kernel, tq=tq, tk=tk, dk=dk, n_sub=n_sub),
        grid=(B, heads, S // (n_sub * tq)),
        in_specs=[smem, smem,
                  pl.BlockSpec((1, n_sub * tq, dv), lambda b, h, i: (b, i, h)),
                  pl.BlockSpec((1, S, dv), lambda b, h, i: (b, 0, k_blk + h)),
                  pl.BlockSpec((1, S, dv), lambda b, h, i: (b, 0, v_blk + h)),
                  pl.BlockSpec((1, dv), lambda b, h, i: (0, 0)),
                  pl.BlockSpec((1, dv), lambda b, h, i: (0, 0)),
                  pl.BlockSpec((1, dv), lambda b, h, i: (0, 0))],
        out_specs=pl.BlockSpec((1, n_sub * tq, dv), lambda b, h, i: (b, i, h)),
        out_shape=jax.ShapeDtypeStruct((B, S, heads * dv), BF16),
        scratch_shapes=[pltpu.VMEM((S, dv), BF16), pltpu.VMEM((S, dv), BF16),
                        pltpu.VMEM((S // tk, rows_ext, tk), BF16)] + tile_scratch,
        compiler_params=_cparams(("parallel", "parallel", "arbitrary")),
        name="diff_attn",
    )(scal, slopes, h3, h3, h3, qg, kg, hg)


def _complex_powers(a_re, a_im, n):
    p_re = jnp.ones((1,) + a_re.shape, F32)
    p_im = jnp.zeros((1,) + a_re.shape, F32)
    s_re, s_im = a_re, a_im
    while p_re.shape[0] < n + 1:
        q_re = p_re * s_re - p_im * s_im
        q_im = p_re * s_im + p_im * s_re
        p_re = jnp.concatenate([p_re, q_re], axis=0)
        p_im = jnp.concatenate([p_im, q_im], axis=0)
        s_re, s_im = s_re * s_re - s_im * s_im, 2.0 * s_re * s_im
    return p_re[:n + 1], p_im[:n + 1]


def _s5_operators(lam_re, lam_im, log_dt, b_re, b_im, c_re, c_im, d_skip, L):
    hp = lax.Precision.HIGHEST
    G, N, P = b_re.shape[1], b_re.shape[2], b_re.shape[3]
    LP = L * P
    mats = []
    for d in range(2):
        dt = jnp.exp(log_dt[d].astype(F32))[:, None]
        lr, li = lam_re[d].astype(F32), lam_im[d].astype(F32)
        mag = jnp.exp(lr * dt)
        a_re, a_im = mag * jnp.cos(li * dt), mag * jnp.sin(li * dt)
        nr = a_re - 1.0
        den = lr * lr + li * li
        f_re = (nr * lr + a_im * li) / den
        f_im = (a_im * lr - nr * li) / den
        br, bi = b_re[d].astype(F32), b_im[d].astype(F32)
        bb_re = f_re[..., None] * br - f_im[..., None] * bi
        bb_im = f_re[..., None] * bi + f_im[..., None] * br
        pw_re, pw_im = _complex_powers(a_re, a_im, L)
        cr, ci = c_re[d].astype(F32), c_im[d].astype(F32)
        cv_re = cr[None] * pw_re[:, :, None, :] - ci[None] * pw_im[:, :, None, :]
        cv_im = cr[None] * pw_im[:, :, None, :] + ci[None] * pw_re[:, :, None, :]
        kern = (jnp.einsum('tgpn,gnq->tgpq', cv_re[:L], bb_re, precision=hp)
                - jnp.einsum('tgpn,gnq->tgpq', cv_im[:L], bb_im, precision=hp))
        ab_re = pw_re[:L, :, :, None] * bb_re[None] - pw_im[:L, :, :, None] * bb_im[None]
        ab_im = pw_re[:L, :, :, None] * bb_im[None] + pw_im[:L, :, :, None] * bb_re[None]
        mats.append(dict(kern=kern, ab_re=ab_re, ab_im=ab_im, cv_re=cv_re, cv_im=cv_im,
                         aL_re=pw_re[L], aL_im=pw_im[L]))
    f, b = mats
    dsk = d_skip.astype(F32).reshape(G, P)
    k0 = f['kern'][0] + b['kern'][0] + dsk[:, :, None] * jnp.eye(P, dtype=F32)[None]
    kfull = jnp.concatenate([b['kern'][1:][::-1], k0[None], f['kern'][1:]], axis=0)
    idx = (jnp.arange(L)[None, :] - jnp.arange(L)[:, None]) + (L - 1)
    m5 = kfull[idx]
    mt = m5.transpose(2, 0, 4, 1, 3).reshape(G, LP, LP).astype(BF16)

    zeros_sn = jnp.zeros((G, LP, N), F32)

    def st_block(ab, order):
        return ab[order].transpose(1, 0, 3, 2).reshape(G, LP, N)

    rev = jnp.arange(L)[::-1]
    fwd = jnp.arange(L)
    wst = jnp.concatenate([st_block(f['ab_re'], rev), zeros_sn, st_block(f['ab_im'], rev), zeros_sn,
                           st_block(b['ab_re'], fwd), zeros_sn, st_block(b['ab_im'], fwd), zeros_sn],
                          axis=2).astype(BF16)

    zeros_ns = jnp.zeros((G, N, LP), F32)

    def out_block(cv, taus):
        return cv[taus].transpose(1, 3, 0, 2).reshape(G, N, LP)

    tf = jnp.arange(1, L + 1)
    tb = L - jnp.arange(L)
    wout = jnp.concatenate([out_block(f['cv_re'], tf), zeros_ns, -out_block(f['cv_im'], tf), zeros_ns,
                            out_block(b['cv_re'], tb), zeros_ns, -out_block(b['cv_im'], tb), zeros_ns],
                           axis=1).astype(BF16)

    pad = jnp.zeros((G, N), F32)
    rows = [jnp.concatenate([v, pad], axis=1) for v in
            (f['aL_re'], f['aL_im'], b['aL_re'], b['aL_im'])]
    apow = jnp.stack(rows + [jnp.zeros((G, 2 * N), F32)] * 4, axis=1)
    return mt, wst, wout, apow


def _gelu_tanh(y):
    c = math.sqrt(2.0 / math.pi)
    return 0.5 * y * (1.0 + jnp.tanh(c * (y + 0.044715 * (y * y * y))))


def _s5_kernel(ut_ref, mt_ref, wst_ref, wout_ref, ap_ref, y_ref, s_sc, p_sc, *, nc, rows, n2):
    ut = ut_ref[0]
    s_sc[...] = jnp.dot(ut, wst_ref[0], preferred_element_type=F32)
    zero = jnp.zeros((rows, n2), F32)

    def direction(col0, ar, ai, reverse):
        def step(k, carry):
            h_re, h_im = carry
            c = (nc - 1 - k) if reverse else k
            r = pl.ds(pl.multiple_of(c * rows, rows), rows)
            p_sc[r, col0:col0 + n2] = h_re
            p_sc[r, col0 + n2:col0 + 2 * n2] = h_im
            s_re = s_sc[r, col0:col0 + n2]
            s_im = s_sc[r, col0 + n2:col0 + 2 * n2]
            return (ar * h_re - ai * h_im + s_re, ar * h_im + ai * h_re + s_im)
        lax.fori_loop(0, nc, step, (zero, zero))

    direction(0, ap_ref[0, 0:1, :], ap_ref[0, 1:2, :], False)
    direction(2 * n2, ap_ref[0, 2:3, :], ap_ref[0, 3:4, :], True)

    y = (jnp.dot(ut, mt_ref[0], preferred_element_type=F32)
         + jnp.dot(p_sc[...].astype(BF16), wout_ref[0], preferred_element_type=F32))
    y_ref[0] = _gelu_tanh(y).astype(y_ref.dtype)


def _s5_chunked(ut, mt, wst, wout, apow, *, nc, rows):
    G, cols, LP = ut.shape
    n8 = wst.shape[2]
    n2 = n8 // 4
    return pl.pallas_call(
        functools.partial(_s5_kernel, nc=nc, rows=rows, n2=n2),
        grid=(G,),
        in_specs=[pl.BlockSpec((1, cols, LP), lambda g: (g, 0, 0)),
                  pl.BlockSpec((1, LP, LP), lambda g: (g, 0, 0)),
                  pl.BlockSpec((1, LP, n8), lambda g: (g, 0, 0)),
                  pl.BlockSpec((1, n8, LP), lambda g: (g, 0, 0)),
                  pl.BlockSpec((1, SUBLANES, n2), lambda g: (g, 0, 0))],
        out_specs=pl.BlockSpec((1, cols, LP), lambda g: (g, 0, 0)),
        out_shape=jax.ShapeDtypeStruct((G, cols, LP), BF16),
        scratch_shapes=[pltpu.VMEM((cols, n8), F32), pltpu.VMEM((cols, n8), F32)],
        compiler_params=_cparams(("parallel",)),
        name="s5_chunked",
    )(ut, mt, wst, wout, apow)


def _glu_merge_kernel(y_ref, o_ref, ga_ref, gb_ref, wg_ref, wa_ref, wb_ref, out_ref):
    y = y_ref[...]
    z = jnp.dot(y, wg_ref[...], preferred_element_type=F32)
    s = (y.astype(F32) * _sigmoid(z)).astype(BF16)
    a_out = jnp.dot(o_ref[...], wa_ref[...], preferred_element_type=F32)
    b_out = jnp.dot(s, wb_ref[...], preferred_element_type=F32)
    merged = (_sigmoid(ga_ref[...].astype(F32)) * a_out + _sigmoid(gb_ref[...].astype(F32)) * b_out)
    out_ref[...] = merged.astype(out_ref.dtype)


def _glu_merge(y, o, h, wg, wa, wb, *, ga_blk, gb_blk, tm=256):
    T, SW = y.shape
    AW = o.shape[1]
    D = wa.shape[1]
    tm = min(tm, T)
    return pl.pallas_call(
        _glu_merge_kernel,
        grid=(T // tm,),
        in_specs=[pl.BlockSpec((tm, SW), lambda i: (i, 0)),
                  pl.BlockSpec((tm, AW), lambda i: (i, 0)),
                  pl.BlockSpec((tm, D), lambda i: (i, ga_blk)),
                  pl.BlockSpec((tm, D), lambda i: (i, gb_blk)),
                  pl.BlockSpec((SW, SW), lambda i: (0, 0)),
                  pl.BlockSpec((AW, D), lambda i: (0, 0)),
                  pl.BlockSpec((SW, D), lambda i: (0, 0))],
        out_specs=pl.BlockSpec((tm, D), lambda i: (i, 0)),
        out_shape=jax.ShapeDtypeStruct((T, D), BF16),
        compiler_params=_cparams(("parallel",)),
        name="glu_merge",
    )(y, o, h, h, wg, wa, wb)


def _outproj_kernel(m_ref, x_ref, w_ref, o_ref):
    o_ref[...] = x_ref[...] + jnp.dot(m_ref[...], w_ref[...], preferred_element_type=F32)


def _outproj(merged, x2d, w, *, tm=512):
    T, D = x2d.shape
    tm = min(tm, T)
    return pl.pallas_call(
        _outproj_kernel,
        grid=(T // tm,),
        in_specs=[pl.BlockSpec((tm, D), lambda i: (i, 0)),
                  pl.BlockSpec((tm, D), lambda i: (i, 0)),
                  pl.BlockSpec((D, D), lambda i: (0, 0))],
        out_specs=pl.BlockSpec((tm, D), lambda i: (i, 0)),
        out_shape=jax.ShapeDtypeStruct((T, D), F32),
        compiler_params=_cparams(("parallel",)),
        name="outproj",
    )(merged, x2d, w)


def _router_kernel(x_ref, g_ref, whi_ref, wlo_ref, b_ref, xe_ref, meta_ref, cnt_ref,
                   tri_sc, eye_sc, carry_sc, *, n_groups, per_group, e_row0):
    t = pl.program_id(0)
    tm, D = x_ref.shape
    n_rows = whi_ref.shape[0]

    @pl.when(t == 0)
    def _():
        r = lax.broadcasted_iota(jnp.int32, (tm, tm), 0)
        c = lax.broadcasted_iota(jnp.int32, (tm, tm), 1)
        tri_sc[...] = (r < c).astype(BF16)
        eye_sc[...] = (r == c).astype(BF16)
        carry_sc[...] = jnp.zeros(carry_sc.shape, F32)

    x = x_ref[...]
    ms = jnp.mean(x * x, axis=-1, keepdims=True)
    xn = x * lax.rsqrt(ms + RMS_EPS) * g_ref[...]
    xe_ref[:, 0:D] = xn

    nt = (((1,), (1,)), ((), ()))
    x_hi, x_lo = _split_bf16(xn)
    whi, wlo = whi_ref[...], wlo_ref[...]
    logits = (lax.dot_general(whi, x_hi, nt, preferred_element_type=F32)
              + lax.dot_general(whi, x_lo, nt, preferred_element_type=F32)
              + lax.dot_general(wlo, x_hi, nt, preferred_element_type=F32)) + b_ref[...]

    g = [logits[k:k + 1, :] for k in range(n_groups)]
    g_max = functools.reduce(jnp.maximum, g)
    g_idx = jnp.full(g_max.shape, n_groups - 1, jnp.int32)
    for k in range(n_groups - 2, -1, -1):
        g_idx = jnp.where(g[k] == g_max, k, g_idx)
    g_w = 1.0 / functools.reduce(lambda a, b: a + b, [jnp.exp(v - g_max) for v in g])

    e_in = []
    for j in range(per_group):
        v = logits[e_row0 + j:e_row0 + j + 1, :]
        for k in range(1, n_groups):
            row = e_row0 + k * per_group + j
            v = jnp.where(g_idx == k, logits[row:row + 1, :], v)
        e_in.append(v)

    def first_argmax(vals):
        vmax = functools.reduce(jnp.maximum, vals)
        idx = jnp.full(vmax.shape, len(vals) - 1, jnp.int32)
        for k in range(len(vals) - 2, -1, -1):
            idx = jnp.where(vals[k] == vmax, k, idx)
        return vmax, idx

    v1, i1 = first_argmax(e_in)
    rest = [jnp.where(i1 == j, -jnp.inf, e_in[j]) for j in range(per_group)]
    v2, i2 = first_argmax(rest)
    e2 = jnp.exp(v2 - v1)
    w1 = g_w / (1.0 + e2)
    w2 = g_w * e2 / (1.0 + e2)
    lo_first = i1 < i2
    a = jnp.where(lo_first, i1, i2)
    b = jnp.where(lo_first, i2, i1)
    w_a = jnp.where(lo_first, w1, w2)
    w_b = jnp.where(lo_first, w2, w1)
    pair = jnp.where(a == 0, b - 1, jnp.where(a == 1, b + 1, 5))
    bucket = g_idx * N_PAIRS + pair

    rows = lax.broadcasted_iota(jnp.int32, (n_rows, tm), 0)
    onehot = (rows == bucket).astype(F32)
    prefix = jnp.dot(onehot.astype(BF16), tri_sc[...], preferred_element_type=F32)
    rank = jnp.sum(onehot * (prefix + carry_sc[...]), axis=0, keepdims=True)
    carry_sc[...] = carry_sc[...] + jnp.sum(onehot, axis=1, keepdims=True)
    cnt_ref[...] = jnp.broadcast_to(carry_sc[...], cnt_ref.shape)

    meta_ref[0, 0:1, :] = bucket.astype(F32)
    meta_ref[0, 1:2, :] = rank
    meta_ref[0, 2:3, :] = w_a
    meta_ref[0, 3:4, :] = w_b
    meta_ref[0, 4:8, :] = jnp.zeros((4, tm), F32)

    eye = eye_sc[...]
    for blk, w in enumerate((w_a, w_b)):
        wrep = jnp.broadcast_to(w, (LANES, tm))
        hi, lo = _split_bf16(wrep)
        col = (lax.dot_general(eye, hi, nt, preferred_element_type=F32)
               + lax.dot_general(eye, lo, nt, preferred_element_type=F32))
        xe_ref[:, D + blk * LANES:D + (blk + 1) * LANES] = col


def _router(x2d, gain, w_rg, b_rg, w_re, b_re, *, tm=512):
    T, D = x2d.shape
    n_groups = w_rg.shape[1]
    n_exp = w_re.shape[1]
    per_group = n_exp // n_groups
    e_row0 = SUBLANES
    n_rows = e_row0 + n_exp
    n_rows = ((n_rows + SUBLANES - 1) // SUBLANES) * SUBLANES
    assert n_groups * N_PAIRS <= n_rows and per_group == 4
    wt = jnp.zeros((n_rows, D), F32)
    wt = wt.at[0:n_groups].set(w_rg.astype(F32).T).at[e_row0:e_row0 + n_exp].set(w_re.astype(F32).T)
    bias = jnp.zeros((n_rows, 1), F32)
    bias = bias.at[0:n_groups, 0].set(b_rg.astype(F32)).at[e_row0:e_row0 + n_exp, 0].set(b_re.astype(F32))
    w_hi, w_lo = _split_bf16(wt)
    tm = min(tm, T)
    De = D + 2 * LANES
    return pl.pallas_call(
        functools.partial(_router_kernel, n_groups=n_groups, per_group=per_group, e_row0=e_row0),
        grid=(T // tm,),
        in_specs=[pl.BlockSpec((tm, D), lambda t: (t, 0)),
                  pl.BlockSpec((1, D), lambda t: (0, 0)),
                  pl.BlockSpec((n_rows, D), lambda t: (0, 0)),
                  pl.BlockSpec((n_rows, D), lambda t: (0, 0)),
                  pl.BlockSpec((n_rows, 1), lambda t: (0, 0))],
        out_specs=[pl.BlockSpec((tm, De), lambda t: (t, 0)),
                   pl.BlockSpec((1, SUBLANES, tm), lambda t: (t, 0, 0)),
                   pl.BlockSpec((n_rows, LANES), lambda t: (0, 0))],
        out_shape=[jax.ShapeDtypeStruct((T, De), F32),
                   jax.ShapeDtypeStruct((T // tm, SUBLANES, tm), F32),
                   jax.ShapeDtypeStruct((n_rows, LANES), F32)],
        scratch_shapes=[pltpu.VMEM((tm, tm), BF16), pltpu.VMEM((tm, tm), BF16),
                        pltpu.VMEM((n_rows, 1), F32)],
        compiler_params=_cparams(("arbitrary",)),
        name="router",
    )(x2d, gain.reshape(1, D).astype(F32), w_hi, w_lo, bias)


def _dispatch_kernel(pos_ref, x_ref, xs_in_ref, xs_ref, sem, *, tm):
    del xs_in_ref
    base = pl.program_id(0) * tm

    def issue(r, carry):
        pltpu.make_async_copy(x_ref.at[pl.ds(r, 1)], xs_ref.at[pl.ds(pos_ref[base + r], 1)], sem).start()
        return carry

    lax.fori_loop(0, tm, issue, 0, unroll=8)

    def drain(r, carry):
        pltpu.make_async_copy(x_ref.at[pl.ds(0, 1)], xs_ref.at[pl.ds(0, 1)], sem).wait()
        return carry

    lax.fori_loop(0, tm, drain, 0, unroll=8)


def _dispatch(pos, xe, n_rows, *, tm=512):
    T, De = xe.shape
    tm = min(tm, T)
    xs0 = jnp.zeros((n_rows, De), F32)
    return pl.pallas_call(
        functools.partial(_dispatch_kernel, tm=tm),
        grid_spec=pltpu.PrefetchScalarGridSpec(
            num_scalar_prefetch=1,
            grid=(T // tm,),
            in_specs=[pl.BlockSpec((tm, De), lambda t, pos: (t, 0)),
                      pl.BlockSpec(memory_space=pl.ANY)],
            out_specs=pl.BlockSpec(memory_space=pl.ANY),
            scratch_shapes=[pltpu.SemaphoreType.DMA(())]),
        out_shape=jax.ShapeDtypeStruct((n_rows, De), F32),
        input_output_aliases={2: 0},
        compiler_params=_cparams(("arbitrary",)),
        name="moe_dispatch",
    )(pos, xe, xs0)


def _collect_kernel(pos_ref, ys_ref, x_ref, o_ref, buf, sem, *, tm):
    base = pl.program_id(0) * tm

    def issue(r, carry):
        pltpu.make_async_copy(ys_ref.at[pl.ds(pos_ref[base + r], 1)], buf.at[pl.ds(r, 1)], sem).start()
        return carry

    lax.fori_loop(0, tm, issue, 0, unroll=8)

    def drain(r, carry):
        pltpu.make_async_copy(ys_ref.at[pl.ds(0, 1)], buf.at[pl.ds(0, 1)], sem).wait()
        return carry

    lax.fori_loop(0, tm, drain, 0, unroll=8)
    o_ref[...] = x_ref[...] + buf[...]


def _collect(pos, ys, x2d, *, tm=512):
    T, D = x2d.shape
    tm = min(tm, T)
    return pl.pallas_call(
        functools.partial(_collect_kernel, tm=tm),
        grid_spec=pltpu.PrefetchScalarGridSpec(
            num_scalar_prefetch=1,
            grid=(T // tm,),
            in_specs=[pl.BlockSpec(memory_space=pl.ANY),
                      pl.BlockSpec((tm, D), lambda t, pos: (t, 0))],
            out_specs=pl.BlockSpec((tm, D), lambda t, pos: (t, 0)),
            scratch_shapes=[pltpu.VMEM((tm, D), F32), pltpu.SemaphoreType.DMA(())]),
        out_shape=jax.ShapeDtypeStruct((T, D), F32),
        compiler_params=_cparams(("arbitrary",)),
        name="moe_collect",
    )(pos, ys, x2d)


def _ffn_kernel(te_ref, tv_ref, x_ref, w1_ref, w3_ref, w2_ref, o_ref, xb_sc, *, D):
    i, j, f = pl.program_id(0), pl.program_id(1), pl.program_id(2)
    first = jnp.logical_and(j == 0, f == 0)

    @pl.when(first)
    def _():
        xb_sc[...] = x_ref[:, 0:D].astype(BF16)
        o_ref[...] = jnp.zeros(o_ref.shape, F32)

    @pl.when(tv_ref[i] != 0)
    def _():
        xb = xb_sc[...]
        h1 = jnp.dot(xb, w1_ref[0], preferred_element_type=F32)
        h3 = jnp.dot(xb, w3_ref[0], preferred_element_type=F32)
        hid = (h1 * _sigmoid(h1) * h3).astype(BF16)
        y = jnp.dot(hid, w2_ref[0], preferred_element_type=F32)
        w = jnp.where(j == 0, x_ref[:, D:D + 1], x_ref[:, D + LANES:D + LANES + 1])
        o_ref[...] += w * y


def _expert_ffn(tile_expert, tile_valid, xs, w1, w3, w2, *, tm, tf=512):
    R, De = xs.shape
    E, D, FF = w1.shape
    tf = min(tf, FF)
    n_tiles = R // tm
    return pl.pallas_call(
        functools.partial(_ffn_kernel, D=D),
        grid_spec=pltpu.PrefetchScalarGridSpec(
            num_scalar_prefetch=2,
            grid=(n_tiles, 2, FF // tf),
            in_specs=[pl.BlockSpec((tm, De), lambda i, j, f, te, tv: (i, 0)),
                      pl.BlockSpec((1, D, tf), lambda i, j, f, te, tv: (te[2 * i + j], 0, f)),
                      pl.BlockSpec((1, D, tf), lambda i, j, f, te, tv: (te[2 * i + j], 0, f)),
                      pl.BlockSpec((1, tf, D), lambda i, j, f, te, tv: (te[2 * i + j], f, 0))],
            out_specs=pl.BlockSpec((tm, D), lambda i, j, f, te, tv: (i, 0)),
            scratch_shapes=[pltpu.VMEM((tm, D), BF16)]),
        out_shape=jax.ShapeDtypeStruct((R, D), F32),
        compiler_params=_cparams(("parallel", "arbitrary", "arbitrary")),
        name="expert_ffn",
    )(tile_expert, tile_valid, xs, w1, w3, w2)


def _moe_plan(meta, counts, n_groups, per_group, T, tm_e):
    n_buckets = n_groups * N_PAIRS
    bucket = meta[:, 0, :].reshape(T).astype(jnp.int32)
    rank = meta[:, 1, :].reshape(T).astype(jnp.int32)
    cnt = counts[:n_buckets, 0].astype(jnp.int32)
    padded = ((cnt + tm_e - 1) // tm_e) * tm_e
    ends = jnp.cumsum(padded)
    offs = ends - padded
    pos = offs[bucket] + rank
    n_tiles = T // tm_e + n_buckets
    starts = jnp.arange(n_tiles, dtype=jnp.int32) * tm_e
    tile_valid = (starts < ends[-1]).astype(jnp.int32)
    last_start = jnp.maximum(ends[-1] - tm_e, 0)
    tb = jnp.sum((ends[None, :] <= jnp.minimum(starts, last_start)[:, None]).astype(jnp.int32), axis=1)
    tb = jnp.minimum(tb, n_buckets - 1)
    grp, pair = tb // N_PAIRS, tb % N_PAIRS
    ea = grp * per_group + jnp.asarray(_PAIR_A, jnp.int32)[pair]
    eb = grp * per_group + jnp.asarray(_PAIR_B, jnp.int32)[pair]
    tile_expert = jnp.stack([ea, eb], axis=1).reshape(-1)
    return pos, tile_expert, tile_valid, n_tiles


def _hier_moe_residual(x2d, gain, w_rg, b_rg, w_re, b_re, w1, w3, w2, expert0, *, tm_e=512):
    T, D = x2d.shape
    n_groups = w_rg.shape[1]
    per_group = w_re.shape[1] // n_groups
    tm_e = min(tm_e, T)
    xe, meta, counts = _router(x2d, gain, w_rg, b_rg, w_re, b_re)
    pos, tile_expert, tile_valid, n_tiles = _moe_plan(meta, counts, n_groups, per_group, T, tm_e)
    xs = _dispatch(pos, xe, n_tiles * tm_e)
    ys = _expert_ffn(tile_expert + expert0, tile_valid, xs, w1, w3, w2, tm=tm_e)
    return _collect(pos, ys, x2d)


def _hybrid_mixer_residual(x2d, B, S, layer_idx, slopes, norm1, w_in, q_gain, k_gain, lam_q1, lam_k1,
                           lam_q2, lam_k2, head_gain, w_attn_up, lam_re, lam_im, log_dt, b_re, b_im,
                           c_re, c_im, d_skip, w_glu, w_ssm_up, w_out):
    T, D = x2d.shape
    dk = q_gain.shape[0]
    dv = head_gain.shape[0]
    AW, SW = w_attn_up.shape[0], w_glu.shape[0]
    heads = AW // dv
    assert dv == 2 * dk == LANES
    u_off = 3 * AW
    ga_off = u_off + SW
    gb_off = ga_off + D
    assert ga_off % D == 0

    h = _norm_inproj(x2d, norm1, w_in.astype(BF16))

    lam_init = 0.8 - 0.6 * math.exp(-0.3 * layer_idx)
    lam = (jnp.exp(jnp.sum(lam_q1.astype(F32) * lam_k1.astype(F32)))
           - jnp.exp(jnp.sum(lam_q2.astype(F32) * lam_k2.astype(F32))) + lam_init)
    scal = jnp.stack([lam, jnp.asarray(1.0 - lam_init, F32)]).astype(F32)
    o = _diff_attention(h.reshape(B, S, -1), scal, slopes, q_gain, k_gain, head_gain,
                        heads=heads, dk=dk, k_blk=AW // dv, v_blk=2 * AW // dv)
    o = o.reshape(T, AW)

    G, P = b_re.shape[1], b_re.shape[3]
    L = min(S5_CHUNK, S)
    nc = S // L
    assert B % SUBLANES == 0
    mt, wst, wout, apow = _s5_operators(lam_re, lam_im, log_dt, b_re, b_im, c_re, c_im, d_skip, L)
    u = h[:, u_off:u_off + SW].reshape(B, nc, L, G, P)
    ut = u.transpose(3, 1, 0, 2, 4).reshape(G, nc * B, L * P)
    yt = _s5_chunked(ut, mt, wst, wout, apow, nc=nc, rows=B)
    y = yt.reshape(G, nc, B, L, P).transpose(2, 1, 3, 0, 4).reshape(T, SW)

    merged = _glu_merge(y, o, h, w_glu.astype(BF16), w_attn_up.astype(BF16), w_ssm_up.astype(BF16),
                        ga_blk=ga_off // D, gb_blk=gb_off // D)
    return _outproj(merged, x2d, w_out.astype(BF16))


def kernel(x, norm1, w_in, q_gain, k_gain, lam_q1, lam_k1, lam_q2, lam_k2, head_gain, w_attn_up, ssm_lam_re, ssm_lam_im, ssm_log_dt, ssm_b_re, ssm_b_im, ssm_c_re, ssm_c_im, ssm_d, w_glu, w_ssm_up, w_out, norm2, w_router_group, b_router_group, w_router_expert, b_router_expert, w1, w3, w2):
    B, S, D = x.shape
    depth = norm1.shape[0]
    heads = w_attn_up.shape[1] // head_gain.shape[1]
    slopes = jnp.exp2(-8.0 * jnp.arange(1, heads + 1, dtype=F32) / heads)
    x2d = x.reshape(B * S, D).astype(F32)
    n_exp = w1.shape[1]
    w1b, w3b, w2b = (w.astype(BF16).reshape((depth * n_exp,) + w.shape[2:]) for w in (w1, w3, w2))
    for l in range(depth):
        x2d = _hybrid_mixer_residual(
            x2d, B, S, l, slopes, norm1[l], w_in[l], q_gain[l], k_gain[l], lam_q1[l], lam_k1[l],
            lam_q2[l], lam_k2[l], head_gain[l], w_attn_up[l], ssm_lam_re[l], ssm_lam_im[l],
            ssm_log_dt[l], ssm_b_re[l], ssm_b_im[l], ssm_c_re[l], ssm_c_im[l], ssm_d[l], w_glu[l],
            w_ssm_up[l], w_out[l])
        x2d = _hier_moe_residual(
            x2d, norm2[l], w_router_group[l], b_router_group[l], w_router_expert[l],
            b_router_expert[l], w1b, w3b, w2b, l * n_exp)
    return x2d.reshape(B, S, D).astype(x.dtype)
```

```python
import functools
import math

import jax
import jax.numpy as jnp
from jax import lax
from jax.experimental import pallas as pl
from jax.experimental.pallas import tpu as pltpu

F32 = jnp.float32
BF16 = jnp.bfloat16
RMS_EPS = 1e-6
LOG2E = math.log2(math.e)
LANES = 128
SUBLANES = 8
VMEM_LIMIT = 56 * 1024 * 1024
S5_CHUNK = 8
N_PAIRS = 6
_PAIR_A = (0, 0, 0, 1, 1, 2)
_PAIR_B = (1, 2, 3, 2, 3, 3)


def _cparams(sem):
    return pltpu.CompilerParams(dimension_semantics=sem, vmem_limit_bytes=VMEM_LIMIT)


def _sigmoid(x):
    return 1.0 / (1.0 + jnp.exp(-x))


def _pick_tile(n, target, unit=LANES):
    if n <= target:
        return n
    t = (target // unit) * unit
    while n % t:
        t -= unit
    return t


def _split_bf16(x):
    hi = x.astype(BF16)
    lo = (x - hi.astype(F32)).astype(BF16)
    return hi, lo


def _norm_inproj_kernel(x_ref, g_ref, w_ref, o_ref, xn_ref):
    @pl.when(pl.program_id(1) == 0)
    def _():
        x = x_ref[...]
        ms = jnp.mean(x * x, axis=-1, keepdims=True)
        xn_ref[...] = (x * lax.rsqrt(ms + RMS_EPS) * g_ref[...]).astype(BF16)

    o_ref[...] = jnp.dot(xn_ref[...], w_ref[...], preferred_element_type=F32).astype(o_ref.dtype)


def _norm_inproj(x2d, gain, w, *, tm=1024, tn=1024):
    T, D = x2d.shape
    N = w.shape[1]
    tm, tn = _pick_tile(T, tm, SUBLANES), _pick_tile(N, tn)
    return pl.pallas_call(
        _norm_inproj_kernel,
        grid=(T // tm, N // tn),
        in_specs=[pl.BlockSpec((tm, D), lambda i, j: (i, 0)),
                  pl.BlockSpec((1, D), lambda i, j: (0, 0)),
                  pl.BlockSpec((D, tn), lambda i, j: (0, j))],
        out_specs=pl.BlockSpec((tm, tn), lambda i, j: (i, j)),
        out_shape=jax.ShapeDtypeStruct((T, N), BF16),
        scratch_shapes=[pltpu.VMEM((tm, D), BF16)],
        compiler_params=_cparams(("parallel", "arbitrary")),
        name="norm_inproj",
    )(x2d, gain.reshape(1, D).astype(F32), w)


N_BIAS_LANES = 6


def _attn_kernel(scal_ref, slope_ref, q_ref, k_ref, v_ref, qg_ref, kg_ref, hg_ref, o_ref,
                 ka_ref, kb_ref, vt_ref, *tile_scratch, tq, tk, dk, n_sub):
    k_ref_map = (ka_ref, kb_ref)
    h = pl.program_id(1)
    i = pl.program_id(2)
    S = k_ref.shape[1]
    dv = 2 * dk
    nk = S // tk
    nt = (((1,), (1,)), ((), ()))
    lane = lax.broadcasted_iota(jnp.int32, (1, dv), 1)
    first = lane < dk
    feat_lane = jnp.where(first, lane, lane - dk)
    r = lax.broadcasted_iota(jnp.int32, (dv, dv), 0)
    c = lax.broadcasted_iota(jnp.int32, (dv, dv), 1)
    ones_bd = ((r < dk) == (c < dk)).astype(BF16)
    eye_dv = (r == c).astype(BF16)

    def comp_norm(x, gain):
        sq_hi, sq_lo = _split_bf16(x * x)
        ss = (jnp.dot(sq_hi, ones_bd, preferred_element_type=F32)
              + jnp.dot(sq_lo, ones_bd, preferred_element_type=F32))
        return x * lax.rsqrt(ss * (1.0 / dk) + RMS_EPS) * gain

    @pl.when(i == 0)
    def _():
        kn = comp_norm(k_ref[0].astype(F32), kg_ref[...])
        ds = lax.broadcasted_iota(jnp.int32, (S, dv), 0) & (tk - 1)
        ds_lo = ds & 255
        fl = jnp.broadcast_to(feat_lane, (S, dv))
        feat = jnp.where(fl < 3, ds_lo, jnp.where(fl < N_BIAS_LANES, ds - ds_lo, 0)).astype(F32)
        ka_ref[...] = jnp.where(first, kn, feat).astype(BF16)
        kb_ref[...] = jnp.where(first, feat, kn).astype(BF16)
        for jc in range(nk):
            vt_ref[jc, 0:dv, :] = lax.dot_general(
                eye_dv, v_ref[0, jc * tk:(jc + 1) * tk, :], nt,
                preferred_element_type=F32).astype(BF16)
            vt_ref[jc, dv:dv + 2 * SUBLANES, :] = jnp.ones((2 * SUBLANES, tk), BF16)

    slope2 = slope_ref[h] * LOG2E
    s_full = jnp.full((1, dv), slope2, F32)
    s_hi = s_full.astype(BF16).astype(F32)
    s_mid = (s_full - s_hi).astype(BF16).astype(F32)
    s_lo = (s_full - s_hi - s_mid).astype(BF16).astype(F32)
    q_feat = jnp.where((feat_lane == 0) | (feat_lane == 3), s_hi,
                       jnp.where((feat_lane == 1) | (feat_lane == 4), s_mid,
                                 jnp.where((feat_lane == 2) | (feat_lane == 5), s_lo, 0.0)))

    lam = scal_ref[0]
    out_scale = scal_ref[1]
    rq = lax.broadcasted_iota(jnp.int32, (tq, tq), 0)
    cq = lax.broadcasted_iota(jnp.int32, (tq, tq), 1)
    eye_q = (rq == cq).astype(BF16)
    hg_col = jnp.sum(jnp.where(r == c, jnp.broadcast_to(hg_ref[...], (dv, dv)), 0.0),
                     axis=1, keepdims=True) * out_scale

    m_sets = (tile_scratch[0:2], tile_scratch[2:4])
    acc_sets = (tile_scratch[4:6], tile_scratch[6:8])
    z_slots = (tile_scratch[8:10], tile_scratch[10:12])

    class Tile:
        def __init__(self, sub):
            self.sub = sub
            self.q0 = (i * n_sub + sub) * tq
            self.rows = pl.ds(sub * tq, tq)
            self.m_ref, self.acc_ref = m_sets[sub % 2], acc_sets[sub % 2]
            qn = comp_norm(q_ref[0, self.rows, :].astype(F32), qg_ref[...]) * (LOG2E * dk ** -0.5)
            self.q_plain = (jnp.where(first, qn, 0.0).astype(BF16),
                            jnp.where(first, 0.0, qn).astype(BF16))
            self.q_left = (jnp.where(first, qn, q_feat).astype(BF16),
                           jnp.where(first, q_feat, qn).astype(BF16))
            self.q_right = (jnp.where(first, qn, -q_feat).astype(BF16),
                            jnp.where(first, -q_feat, qn).astype(BF16))
            self.jd = self.q0 // tk
            self.t_row = (jnp.asarray(self.q0, F32)
                          + lax.broadcasted_iota(jnp.int32, (1, tq), 1).astype(F32))

        def chunk_of(self, step):
            if step == 0:
                return self.jd, None
            j = (step - 1) + jnp.where((step - 1) >= self.jd, 1, 0)
            return j, j < self.jd

        def scores(self, step, z_ref):
            j, left = self.chunk_of(step)
            start = pl.multiple_of(j * tk, tk)
            if left is None:
                key = lax.broadcasted_iota(jnp.int32, (tk, tq), 0) + j * tk
                qry = lax.broadcasted_iota(jnp.int32, (tk, tq), 1) + self.q0
                bias = (-slope2) * jnp.abs((qry - key).astype(F32))
            for c_idx in range(2):
                kc = k_ref_map[c_idx][pl.ds(start, tk), :]
                if left is None:
                    z = lax.dot_general(kc, self.q_plain[c_idx], nt,
                                        preferred_element_type=F32) + bias
                else:
                    qv = jnp.where(left, self.q_left[c_idx], self.q_right[c_idx])
                    z = lax.dot_general(kc, qv, nt, preferred_element_type=F32)
                z_ref[c_idx][...] = z

        def softmax_pv(self, step, z_ref):
            j, left = self.chunk_of(step)
            vt = vt_ref[j]
            for c_idx in range(2):
                z = z_ref[c_idx][...]
                if left is None:
                    m_new = jnp.max(z, axis=0, keepdims=True)
                    p = jnp.exp2(z - m_new)
                    self.acc_ref[c_idx][...] = jnp.dot(vt, p.astype(BF16),
                                                       preferred_element_type=F32)
                else:
                    rterm = (jnp.where(left, -slope2, slope2)
                             * (self.t_row - jnp.asarray(j * tk, F32)))
                    m_prev = self.m_ref[c_idx][...]
                    m_new = jnp.maximum(m_prev, jnp.max(z, axis=0, keepdims=True) + rterm)
                    alpha = jnp.exp2(m_prev - m_new)
                    p = jnp.exp2(z - (m_new - rterm))
                    self.acc_ref[c_idx][...] = alpha * self.acc_ref[c_idx][...] + jnp.dot(
                        vt, p.astype(BF16), preferred_element_type=F32)
                self.m_ref[c_idx][...] = m_new

        def finish(self):
            acc = self.acc_ref
            o_t = (acc[0][0:dv, :] / acc[0][dv:dv + 1, :]
                   - lam * (acc[1][0:dv, :] / acc[1][dv:dv + 1, :]))
            ms = jnp.mean(o_t * o_t, axis=0, keepdims=True)
            on_t = (o_t * lax.rsqrt(ms + RMS_EPS) * hg_col).astype(BF16)
            o_ref[0, self.rows, :] = lax.dot_general(
                eye_q, on_t, nt, preferred_element_type=F32).astype(o_ref.dtype)

    items = [(sub, step) for sub in range(n_sub) for step in range(nk)]
    tiles = {0: Tile(0)}
    tiles[0].scores(0, z_slots[0])
    pending_finish = None
    for n, (sub, step) in enumerate(items):
        if n + 2 < len(items) and items[n + 2][1] == 0:
            tiles[items[n + 2][0]] = Tile(items[n + 2][0])
        if n + 1 < len(items):
            sub_n, step_n = items[n + 1]
            if sub_n not in tiles:
                tiles[sub_n] = Tile(sub_n)
            tiles[sub_n].scores(step_n, z_slots[(n + 1) % 2])
        tiles[sub].softmax_pv(step, z_slots[n % 2])
        if pending_finish is not None:
            pending_finish.finish()
            pending_finish = None
        if step == nk - 1:
            pending_finish = tiles[sub]
    pending_finish.finish()


def _diff_attention(h3, scal, slopes, q_gain, k_gain, head_gain, *, heads, dk, k_blk, v_blk,
                    tq=256, tk=512, n_sub=4):
    B, S, _ = h3.shape
    dv = 2 * dk
    tq, tk = min(tq, S), min(tk, S)
    n_sub = min(n_sub, S // tq)
    assert tk % tq == 0 and S % tk == 0
    assert S % (n_sub * tq) == 0
    qg = jnp.tile(q_gain.astype(F32), 2).reshape(1, dv)
    kg = jnp.tile(k_gain.astype(F32), 2).reshape(1, dv)
    hg = head_gain.astype(F32).reshape(1, dv)
    smem = pl.BlockSpec(memory_space=pltpu.SMEM)
    rows_ext = dv + 2 * SUBLANES
    tile_scratch = ([pltpu.VMEM((1, tq), F32)] * 4
                    + [pltpu.VMEM((rows_ext, tq), F32)] * 4
                    + [pltpu.VMEM((tk, tq), F32)] * 4)
    return pl.pallas_call(
        functools.partial(_attn_kernel, tq=tq, tk=tk, dk=dk, n_sub=n_sub),
        grid=(B, heads, S // (n_sub * tq)),
        in_specs=[smem, smem,
                  pl.BlockSpec((1, n_sub * tq, dv), lambda b, h, i: (b, i, h)),
                  pl.BlockSpec((1, S, dv), lambda b, h, i: (b, 0, k_blk + h)),
                  pl.BlockSpec((1, S, dv), lambda b, h, i: (b, 0, v_blk + h)),
                  pl.BlockSpec((1, dv), lambda b, h, i: (0, 0)),
                  pl.BlockSpec((1, dv), lambda b, h, i: (0, 0)),
                  pl.BlockSpec((1, dv), lambda b, h, i: (0, 0))],
        out_specs=pl.BlockSpec((1, n_sub * tq, dv), lambda b, h, i: (b, i, h)),
        out_shape=jax.ShapeDtypeStruct((B, S, heads * dv), BF16),
        scratch_shapes=[pltpu.VMEM((S, dv), BF16), pltpu.VMEM((S, dv), BF16),
                        pltpu.VMEM((S // tk, rows_ext, tk), BF16)] + tile_scratch,
        compiler_params=_cparams(("parallel", "parallel", "arbitrary")),
        name="diff_attn",
    )(scal, slopes, h3, h3, h3, qg, kg, hg)


def _complex_powers(a_re, a_im, n):
    p_re = jnp.ones((1,) + a_re.shape, F32)
    p_im = jnp.zeros((1,) + a_re.shape, F32)
    s_re, s_im = a_re, a_im
    while p_re.shape[0] < n + 1:
        q_re = p_re * s_re - p_im * s_im
        q_im = p_re * s_im + p_im * s_re
        p_re = jnp.concatenate([p_re, q_re], axis=0)
        p_im = jnp.concatenate([p_im, q_im], axis=0)
        s_re, s_im = s_re * s_re - s_im * s_im, 2.0 * s_re * s_im
    return p_re[:n + 1], p_im[:n + 1]


def _s5_operators(lam_re, lam_im, log_dt, b_re, b_im, c_re, c_im, d_skip, L):
    hp = lax.Precision.HIGHEST
    G, N, P = b_re.shape[1], b_re.shape[2], b_re.shape[3]
    LP = L * P
    mats = []
    for d in range(2):
        dt = jnp.exp(log_dt[d].astype(F32))[:, None]
        lr, li = lam_re[d].astype(F32), lam_im[d].astype(F32)
        mag = jnp.exp(lr * dt)
        a_re, a_im = mag * jnp.cos(li * dt), mag * jnp.sin(li * dt)
        nr = a_re - 1.0
        den = lr * lr + li * li
        f_re = (nr * lr + a_im * li) / den
        f_im = (a_im * lr - nr * li) / den
        br, bi = b_re[d].astype(F32), b_im[d].astype(F32)
        bb_re = f_re[..., None] * br - f_im[..., None] * bi
        bb_im = f_re[..., None] * bi + f_im[..., None] * br
        pw_re, pw_im = _complex_powers(a_re, a_im, L)
        cr, ci = c_re[d].astype(F32), c_im[d].astype(F32)
        cv_re = cr[None] * pw_re[:, :, None, :] - ci[None] * pw_im[:, :, None, :]
        cv_im = cr[None] * pw_im[:, :, None, :] + ci[None] * pw_re[:, :, None, :]
        kern = (jnp.einsum('tgpn,gnq->tgpq', cv_re[:L], bb_re, precision=hp)
                - jnp.einsum('tgpn,gnq->tgpq', cv_im[:L], bb_im, precision=hp))
        ab_re = pw_re[:L, :, :, None] * bb_re[None] - pw_im[:L, :, :, None] * bb_im[None]
        ab_im = pw_re[:L, :, :, None] * bb_im[None] + pw_im[:L, :, :, None] * bb_re[None]
        mats.append(dict(kern=kern, ab_re=ab_re, ab_im=ab_im, cv_re=cv_re, cv_im=cv_im,
                         aL_re=pw_re[L], aL_im=pw_im[L]))
    f, b = mats
    A = LANES // P
    O = G // A
    eye_a = jnp.eye(A, dtype=F32)
    dsk = d_skip.astype(F32).reshape(G, P)
    k0 = f['kern'][0] + b['kern'][0] + dsk[:, :, None] * jnp.eye(P, dtype=F32)[None]
    kfull = jnp.concatenate([b['kern'][1:][::-1], k0[None], f['kern'][1:]], axis=0)
    idx = (jnp.arange(L)[None, :] - jnp.arange(L)[:, None]) + (L - 1)
    m6 = kfull[idx].reshape(L, L, O, A, P, P)
    m8 = jnp.einsum('stoapq,ab->osaqtbp', m6, eye_a).reshape(O, L * LANES, L * LANES).astype(BF16)

    def st_block(ab, order):
        x = ab[order].reshape(L, O, A, N, P)
        return jnp.einsum('soanq,ab->osaqbn', x, eye_a).reshape(O, L * LANES, A * N)

    def out_block(cv, taus):
        x = cv[taus].reshape(L, O, A, P, N)
        return jnp.einsum('toapn,ab->oantbp', x, eye_a).reshape(O, A * N, L * LANES)

    rev = jnp.arange(L)[::-1]
    fwd = jnp.arange(L)
    tf = jnp.arange(1, L + 1)
    tb = L - jnp.arange(L)
    ops = []
    for d, order, taus in ((f, rev, tf), (b, fwd, tb)):
        wst = jnp.concatenate([st_block(d['ab_re'], order), st_block(d['ab_im'], order)],
                              axis=2).astype(BF16)
        wout = jnp.concatenate([out_block(d['cv_re'], taus), -out_block(d['cv_im'], taus)],
                               axis=1).astype(BF16)
        apow = jnp.stack([d['aL_re'].reshape(O, A * N), d['aL_im'].reshape(O, A * N)]
                         + [jnp.zeros((O, A * N), F32)] * (SUBLANES - 2), axis=1)
        ops.append((wst, wout, apow))
    return m8, ops[0], ops[1]


def _gelu_tanh(y):
    c = math.sqrt(2.0 / math.pi)
    return 0.5 * y * (1.0 + jnp.tanh(c * (y + 0.044715 * (y * y * y))))


def _s5_pass_kernel(*refs, rows, n_chunks, reverse, with_toeplitz):
    if with_toeplitz:
        x_ref, m_ref, wst_ref, wout_ref, ap_ref, y_ref, h_re_sc, h_im_sc, s_sc, p_sc = refs
    else:
        x_ref, wst_ref, wout_ref, ap_ref, part_ref, y_ref, h_re_sc, h_im_sc, s_sc, p_sc = refs
    n = h_re_sc.shape[1]

    @pl.when(pl.program_id(1) == 0)
    def _():
        h_re_sc[...] = jnp.zeros(h_re_sc.shape, F32)
        h_im_sc[...] = jnp.zeros(h_im_sc.shape, F32)

    x = x_ref[0]
    s_sc[...] = jnp.dot(x, wst_ref[0], preferred_element_type=F32)
    ar = ap_ref[0, 0:1, :]
    ai = ap_ref[0, 1:2, :]

    def step(k, carry):
        h_re, h_im = carry
        c = (n_chunks - 1 - k) if reverse else k
        r = pl.ds(pl.multiple_of(c * rows, rows), rows)
        p_sc[r, 0:n] = h_re
        p_sc[r, n:2 * n] = h_im
        return (ar * h_re - ai * h_im + s_sc[r, 0:n], ar * h_im + ai * h_re + s_sc[r, n:2 * n])

    h_re, h_im = lax.fori_loop(0, n_chunks, step, (h_re_sc[...], h_im_sc[...]))
    h_re_sc[...] = h_re
    h_im_sc[...] = h_im

    y = jnp.dot(p_sc[...].astype(BF16), wout_ref[0], preferred_element_type=F32)
    if with_toeplitz:
        y_ref[0] = y + jnp.dot(x, m_ref[0], preferred_element_type=F32)
    else:
        y_ref[0] = _gelu_tanh(y + part_ref[0]).astype(y_ref.dtype)


def _s5_chunked(x, m8, fwd_ops, bwd_ops, *, rows, tr=512):
    O, R, W = x.shape
    n2 = fwd_ops[0].shape[2]
    tr = min(tr, R)
    nt = R // tr
    scratch = [pltpu.VMEM((rows, n2 // 2), F32), pltpu.VMEM((rows, n2 // 2), F32),
               pltpu.VMEM((tr, n2), F32), pltpu.VMEM((tr, n2), F32)]

    def row_block(order):
        return pl.BlockSpec((1, tr, W), lambda o, r: (o, order(r), 0))

    def per_block(shape):
        return pl.BlockSpec((1,) + shape, lambda o, r: (o, 0, 0))

    wst, wout, apow = fwd_ops
    part = pl.pallas_call(
        functools.partial(_s5_pass_kernel, rows=rows, n_chunks=tr // rows, reverse=False,
                          with_toeplitz=True),
        grid=(O, nt),
        in_specs=[row_block(lambda r: r), per_block((W, W)), per_block((W, n2)),
                  per_block((n2, W)), per_block((SUBLANES, n2 // 2))],
        out_specs=row_block(lambda r: r),
        out_shape=jax.ShapeDtypeStruct((O, R, W), F32),
        scratch_shapes=scratch,
        compiler_params=_cparams(("parallel", "arbitrary")),
        name="s5_forward",
    )(x, m8, wst, wout, apow)
    wst, wout, apow = bwd_ops
    return pl.pallas_call(
        functools.partial(_s5_pass_kernel, rows=rows, n_chunks=tr // rows, reverse=True,
                          with_toeplitz=False),
        grid=(O, nt),
        in_specs=[row_block(lambda r: nt - 1 - r), per_block((W, n2)), per_block((n2, W)),
                  per_block((SUBLANES, n2 // 2)), row_block(lambda r: nt - 1 - r)],
        out_specs=row_block(lambda r: nt - 1 - r),
        out_shape=jax.ShapeDtypeStruct((O, R, W), BF16),
        scratch_shapes=scratch,
        compiler_params=_cparams(("parallel", "arbitrary")),
        name="s5_backward",
    )(x, wst, wout, apow, part)


def _glu_merge_kernel(y_ref, o_ref, ga_ref, gb_ref, wg_ref, wa_ref, wb_ref, out_ref):
    y = y_ref[...]
    z = jnp.dot(y, wg_ref[...], preferred_element_type=F32)
    s = (y.astype(F32) * _sigmoid(z)).astype(BF16)
    a_out = jnp.dot(o_ref[...], wa_ref[...], preferred_element_type=F32)
    b_out = jnp.dot(s, wb_ref[...], preferred_element_type=F32)
    merged = (_sigmoid(ga_ref[...].astype(F32)) * a_out + _sigmoid(gb_ref[...].astype(F32)) * b_out)
    out_ref[...] = merged.astype(out_ref.dtype)


def _glu_merge(y, o, h, wg, wa, wb, *, ga_blk, gb_blk, tm=256):
    T, SW = y.shape
    AW = o.shape[1]
    D = wa.shape[1]
    tm = min(tm, T)
    return pl.pallas_call(
        _glu_merge_kernel,
        grid=(T // tm,),
        in_specs=[pl.BlockSpec((tm, SW), lambda i: (i, 0)),
                  pl.BlockSpec((tm, AW), lambda i: (i, 0)),
                  pl.BlockSpec((tm, D), lambda i: (i, ga_blk)),
                  pl.BlockSpec((tm, D), lambda i: (i, gb_blk)),
                  pl.BlockSpec((SW, SW), lambda i: (0, 0)),
                  pl.BlockSpec((AW, D), lambda i: (0, 0)),
                  pl.BlockSpec((SW, D), lambda i: (0, 0))],
        out_specs=pl.BlockSpec((tm, D), lambda i: (i, 0)),
        out_shape=jax.ShapeDtypeStruct((T, D), BF16),
        compiler_params=_cparams(("parallel",)),
        name="glu_merge",
    )(y, o, h, h, wg, wa, wb)


def _outproj_kernel(m_ref, x_ref, w_ref, o_ref):
    o_ref[...] = x_ref[...] + jnp.dot(m_ref[...], w_ref[...], preferred_element_type=F32)


def _outproj(merged, x2d, w, *, tm=512):
    T, D = x2d.shape
    tm = min(tm, T)
    return pl.pallas_call(
        _outproj_kernel,
        grid=(T // tm,),
        in_specs=[pl.BlockSpec((tm, D), lambda i: (i, 0)),
                  pl.BlockSpec((tm, D), lambda i: (i, 0)),
                  pl.BlockSpec((D, D), lambda i: (0, 0))],
        out_specs=pl.BlockSpec((tm, D), lambda i: (i, 0)),
        out_shape=jax.ShapeDtypeStruct((T, D), F32),
        compiler_params=_cparams(("parallel",)),
        name="outproj",
    )(merged, x2d, w)


def _router_kernel(x_ref, g_ref, whi_ref, wlo_ref, b_ref, xe_ref, meta_ref, cnt_ref,
                   tri_sc, eye_sc, carry_sc, *, n_groups, per_group, e_row0):
    t = pl.program_id(0)
    tm, D = x_ref.shape
    n_rows = whi_ref.shape[0]

    @pl.when(t == 0)
    def _():
        r = lax.broadcasted_iota(jnp.int32, (tm, tm), 0)
        c = lax.broadcasted_iota(jnp.int32, (tm, tm), 1)
        tri_sc[...] = (r < c).astype(BF16)
        eye_sc[...] = (r == c).astype(BF16)
        carry_sc[...] = jnp.zeros(carry_sc.shape, F32)

    x = x_ref[...]
    ms = jnp.mean(x * x, axis=-1, keepdims=True)
    xn = x * lax.rsqrt(ms + RMS_EPS) * g_ref[...]
    xe_ref[:, 0:D] = xn

    nt = (((1,), (1,)), ((), ()))
    x_hi, x_lo = _split_bf16(xn)
    whi, wlo = whi_ref[...], wlo_ref[...]
    logits = (lax.dot_general(whi, x_hi, nt, preferred_element_type=F32)
              + lax.dot_general(whi, x_lo, nt, preferred_element_type=F32)
              + lax.dot_general(wlo, x_hi, nt, preferred_element_type=F32)) + b_ref[...]

    g = [logits[k:k + 1, :] for k in range(n_groups)]
    g_max = functools.reduce(jnp.maximum, g)
    g_idx = jnp.full(g_max.shape, n_groups - 1, jnp.int32)
    for k in range(n_groups - 2, -1, -1):
        g_idx = jnp.where(g[k] == g_max, k, g_idx)
    g_w = 1.0 / functools.reduce(lambda a, b: a + b, [jnp.exp(v - g_max) for v in g])

    e_in = []
    for j in range(per_group):
        v = logits[e_row0 + j:e_row0 + j + 1, :]
        for k in range(1, n_groups):
            row = e_row0 + k * per_group + j
            v = jnp.where(g_idx == k, logits[row:row + 1, :], v)
        e_in.append(v)

    def first_argmax(vals):
        vmax = functools.reduce(jnp.maximum, vals)
        idx = jnp.full(vmax.shape, len(vals) - 1, jnp.int32)
        for k in range(len(vals) - 2, -1, -1):
            idx = jnp.where(vals[k] == vmax, k, idx)
        return vmax, idx

    v1, i1 = first_argmax(e_in)
    rest = [jnp.where(i1 == j, -jnp.inf, e_in[j]) for j in range(per_group)]
    v2, i2 = first_argmax(rest)
    e2 = jnp.exp(v2 - v1)
    w1 = g_w / (1.0 + e2)
    w2 = g_w * e2 / (1.0 + e2)
    lo_first = i1 < i2
    a = jnp.where(lo_first, i1, i2)
    b = jnp.where(lo_first, i2, i1)
    w_a = jnp.where(lo_first, w1, w2)
    w_b = jnp.where(lo_first, w2, w1)
    pair = jnp.where(a == 0, b - 1, jnp.where(a == 1, b + 1, 5))
    bucket = g_idx * N_PAIRS + pair

    rows = lax.broadcasted_iota(jnp.int32, (n_rows, tm), 0)
    onehot = (rows == bucket).astype(F32)
    prefix = jnp.dot(onehot.astype(BF16), tri_sc[...], preferred_element_type=F32)
    rank = jnp.sum(onehot * (prefix + carry_sc[...]), axis=0, keepdims=True)
    carry_sc[...] = carry_sc[...] + jnp.sum(onehot, axis=1, keepdims=True)
    cnt_ref[...] = jnp.broadcast_to(carry_sc[...], cnt_ref.shape)

    meta_ref[0, 0:1, :] = bucket.astype(F32)
    meta_ref[0, 1:2, :] = rank
    meta_ref[0, 2:3, :] = w_a
    meta_ref[0, 3:4, :] = w_b
    meta_ref[0, 4:8, :] = jnp.zeros((4, tm), F32)

    eye = eye_sc[...]
    for blk, w in enumerate((w_a, w_b)):
        wrep = jnp.broadcast_to(w, (LANES, tm))
        hi, lo = _split_bf16(wrep)
        col = (lax.dot_general(eye, hi, nt, preferred_element_type=F32)
               + lax.dot_general(eye, lo, nt, preferred_element_type=F32))
        xe_ref[:, D + blk * LANES:D + (blk + 1) * LANES] = col


def _router(x2d, gain, w_rg, b_rg, w_re, b_re, *, tm=512):
    T, D = x2d.shape
    n_groups = w_rg.shape[1]
    n_exp = w_re.shape[1]
    per_group = n_exp // n_groups
    e_row0 = SUBLANES
    n_rows = e_row0 + n_exp
    n_rows = ((n_rows + SUBLANES - 1) // SUBLANES) * SUBLANES
    assert n_groups * N_PAIRS <= n_rows and per_group == 4
    wt = jnp.zeros((n_rows, D), F32)
    wt = wt.at[0:n_groups].set(w_rg.astype(F32).T).at[e_row0:e_row0 + n_exp].set(w_re.astype(F32).T)
    bias = jnp.zeros((n_rows, 1), F32)
    bias = bias.at[0:n_groups, 0].set(b_rg.astype(F32)).at[e_row0:e_row0 + n_exp, 0].set(b_re.astype(F32))
    w_hi, w_lo = _split_bf16(wt)
    tm = min(tm, T)
    De = D + 2 * LANES
    return pl.pallas_call(
        functools.partial(_router_kernel, n_groups=n_groups, per_group=per_group, e_row0=e_row0),
        grid=(T // tm,),
        in_specs=[pl.BlockSpec((tm, D), lambda t: (t, 0)),
                  pl.BlockSpec((1, D), lambda t: (0, 0)),
                  pl.BlockSpec((n_rows, D), lambda t: (0, 0)),
                  pl.BlockSpec((n_rows, D), lambda t: (0, 0)),
                  pl.BlockSpec((n_rows, 1), lambda t: (0, 0))],
        out_specs=[pl.BlockSpec((tm, De), lambda t: (t, 0)),
                   pl.BlockSpec((1, SUBLANES, tm), lambda t: (t, 0, 0)),
                   pl.BlockSpec((n_rows, LANES), lambda t: (0, 0))],
        out_shape=[jax.ShapeDtypeStruct((T, De), F32),
                   jax.ShapeDtypeStruct((T // tm, SUBLANES, tm), F32),
                   jax.ShapeDtypeStruct((n_rows, LANES), F32)],
        scratch_shapes=[pltpu.VMEM((tm, tm), BF16), pltpu.VMEM((tm, tm), BF16),
                        pltpu.VMEM((n_rows, 1), F32)],
        compiler_params=_cparams(("arbitrary",)),
        name="router",
    )(x2d, gain.reshape(1, D).astype(F32), w_hi, w_lo, bias)


def _dispatch_kernel(pos_ref, x_ref, xs_in_ref, xs_ref, sem, *, tm):
    del xs_in_ref
    base = pl.program_id(0) * tm

    def issue(r2, carry):
        for lane in range(2):
            r = 2 * r2 + lane
            pltpu.make_async_copy(x_ref.at[pl.ds(r, 1)], xs_ref.at[pl.ds(pos_ref[base + r], 1)],
                                  sem).start(priority=lane)
        return carry

    lax.fori_loop(0, tm // 2, issue, 0, unroll=4)

    def drain(r, carry):
        pltpu.make_async_copy(x_ref.at[pl.ds(0, 1)], xs_ref.at[pl.ds(0, 1)], sem).wait()
        return carry

    lax.fori_loop(0, tm, drain, 0, unroll=8)


def _dispatch(pos, xe, n_rows, *, tm=512):
    T, De = xe.shape
    tm = min(tm, T)
    xs0 = jnp.zeros((n_rows, De), F32)
    return pl.pallas_call(
        functools.partial(_dispatch_kernel, tm=tm),
        grid_spec=pltpu.PrefetchScalarGridSpec(
            num_scalar_prefetch=1,
            grid=(T // tm,),
            in_specs=[pl.BlockSpec((tm, De), lambda t, pos: (t, 0)),
                      pl.BlockSpec(memory_space=pl.ANY)],
            out_specs=pl.BlockSpec(memory_space=pl.ANY),
            scratch_shapes=[pltpu.SemaphoreType.DMA(())]),
        out_shape=jax.ShapeDtypeStruct((n_rows, De), F32),
        input_output_aliases={2: 0},
        compiler_params=_cparams(("arbitrary",)),
        name="moe_dispatch",
    )(pos, xe, xs0)


def _collect_kernel(pos_ref, ys_ref, x_ref, o_ref, buf, sem, *, tm):
    base = pl.program_id(0) * tm

    def issue(r2, carry):
        for lane in range(2):
            r = 2 * r2 + lane
            pltpu.make_async_copy(ys_ref.at[pl.ds(pos_ref[base + r], 1)], buf.at[pl.ds(r, 1)],
                                  sem).start(priority=lane)
        return carry

    lax.fori_loop(0, tm // 2, issue, 0, unroll=4)

    def drain(r, carry):
        pltpu.make_async_copy(ys_ref.at[pl.ds(0, 1)], buf.at[pl.ds(0, 1)], sem).wait()
        return carry

    lax.fori_loop(0, tm, drain, 0, unroll=8)
    o_ref[...] = x_ref[...] + buf[...]


def _collect(pos, ys, x2d, *, tm=512):
    T, D = x2d.shape
    tm = min(tm, T)
    return pl.pallas_call(
        functools.partial(_collect_kernel, tm=tm),
        grid_spec=pltpu.PrefetchScalarGridSpec(
            num_scalar_prefetch=1,
            grid=(T // tm,),
            in_specs=[pl.BlockSpec(memory_space=pl.ANY),
                      pl.BlockSpec((tm, D), lambda t, pos: (t, 0))],
            out_specs=pl.BlockSpec((tm, D), lambda t, pos: (t, 0)),
            scratch_shapes=[pltpu.VMEM((tm, D), F32), pltpu.SemaphoreType.DMA(())]),
        out_shape=jax.ShapeDtypeStruct((T, D), F32),
        compiler_params=_cparams(("arbitrary",)),
        name="moe_collect",
    )(pos, ys, x2d)


def _ffn_kernel(te_ref, tv_ref, x_ref, w1_ref, w3_ref, w2_ref, o_ref, xb_sc, *, D):
    i, j, f = pl.program_id(0), pl.program_id(1), pl.program_id(2)
    first = jnp.logical_and(j == 0, f == 0)

    @pl.when(first)
    def _():
        xb_sc[...] = x_ref[:, 0:D].astype(BF16)
        o_ref[...] = jnp.zeros(o_ref.shape, F32)

    @pl.when(tv_ref[i] != 0)
    def _():
        xb = xb_sc[...]
        h1 = jnp.dot(xb, w1_ref[0], preferred_element_type=F32)
        h3 = jnp.dot(xb, w3_ref[0], preferred_element_type=F32)
        hid = (h1 * _sigmoid(h1) * h3).astype(BF16)
        y = jnp.dot(hid, w2_ref[0], preferred_element_type=F32)
        w = jnp.where(j == 0, x_ref[:, D:D + 1], x_ref[:, D + LANES:D + LANES + 1])
        o_ref[...] += w * y


def _expert_ffn(tile_expert, tile_valid, xs, w1, w3, w2, *, tm, tf=512):
    R, De = xs.shape
    E, D, FF = w1.shape
    tf = min(tf, FF)
    n_tiles = R // tm
    return pl.pallas_call(
        functools.partial(_ffn_kernel, D=D),
        grid_spec=pltpu.PrefetchScalarGridSpec(
            num_scalar_prefetch=2,
            grid=(n_tiles, 2, FF // tf),
            in_specs=[pl.BlockSpec((tm, De), lambda i, j, f, te, tv: (i, 0)),
                      pl.BlockSpec((1, D, tf), lambda i, j, f, te, tv: (te[2 * i + j], 0, f)),
                      pl.BlockSpec((1, D, tf), lambda i, j, f, te, tv: (te[2 * i + j], 0, f)),
                      pl.BlockSpec((1, tf, D), lambda i, j, f, te, tv: (te[2 * i + j], f, 0))],
            out_specs=pl.BlockSpec((tm, D), lambda i, j, f, te, tv: (i, 0)),
            scratch_shapes=[pltpu.VMEM((tm, D), BF16)]),
        out_shape=jax.ShapeDtypeStruct((R, D), F32),
        compiler_params=_cparams(("parallel", "arbitrary", "arbitrary")),
        name="expert_ffn",
    )(tile_expert, tile_valid, xs, w1, w3, w2)


def _moe_plan(meta, counts, n_groups, per_group, T, tm_e):
    n_buckets = n_groups * N_PAIRS
    bucket = meta[:, 0, :].reshape(T).astype(jnp.int32)
    rank = meta[:, 1, :].reshape(T).astype(jnp.int32)
    cnt = counts[:n_buckets, 0].astype(jnp.int32)
    padded = ((cnt + tm_e - 1) // tm_e) * tm_e
    ends = jnp.cumsum(padded)
    offs = ends - padded
    pos = offs[bucket] + rank
    n_tiles = T // tm_e + n_buckets
    starts = jnp.arange(n_tiles, dtype=jnp.int32) * tm_e
    tile_valid = (starts < ends[-1]).astype(jnp.int32)
    last_start = jnp.maximum(ends[-1] - tm_e, 0)
    tb = jnp.sum((ends[None, :] <= jnp.minimum(starts, last_start)[:, None]).astype(jnp.int32), axis=1)
    tb = jnp.minimum(tb, n_buckets - 1)
    grp, pair = tb // N_PAIRS, tb % N_PAIRS
    ea = grp * per_group + jnp.asarray(_PAIR_A, jnp.int32)[pair]
    eb = grp * per_group + jnp.asarray(_PAIR_B, jnp.int32)[pair]
    tile_expert = jnp.stack([ea, eb], axis=1).reshape(-1)
    return pos, tile_expert, tile_valid, n_tiles


def _hier_moe_residual(x2d, gain, w_rg, b_rg, w_re, b_re, w1, w3, w2, expert0, *, tm_e=512):
    T, D = x2d.shape
    n_groups = w_rg.shape[1]
    per_group = w_re.shape[1] // n_groups
    tm_e = min(tm_e, T)
    xe, meta, counts = _router(x2d, gain, w_rg, b_rg, w_re, b_re)
    pos, tile_expert, tile_valid, n_tiles = _moe_plan(meta, counts, n_groups, per_group, T, tm_e)
    xs = _dispatch(pos, xe, n_tiles * tm_e)
    ys = _expert_ffn(tile_expert + expert0, tile_valid, xs, w1, w3, w2, tm=tm_e)
    return _collect(pos, ys, x2d)


def _hybrid_mixer_residual(x2d, B, S, layer_idx, slopes, norm1, w_in, q_gain, k_gain, lam_q1, lam_k1,
                           lam_q2, lam_k2, head_gain, w_attn_up, lam_re, lam_im, log_dt, b_re, b_im,
                           c_re, c_im, d_skip, w_glu, w_ssm_up, w_out):
    T, D = x2d.shape
    dk = q_gain.shape[0]
    dv = head_gain.shape[0]
    AW, SW = w_attn_up.shape[0], w_glu.shape[0]
    heads = AW // dv
    assert dv == 2 * dk == LANES
    u_off = 3 * AW
    ga_off = u_off + SW
    gb_off = ga_off + D
    assert ga_off % D == 0

    h = _norm_inproj(x2d, norm1, w_in.astype(BF16))

    lam_init = 0.8 - 0.6 * math.exp(-0.3 * layer_idx)
    lam = (jnp.exp(jnp.sum(lam_q1.astype(F32) * lam_k1.astype(F32)))
           - jnp.exp(jnp.sum(lam_q2.astype(F32) * lam_k2.astype(F32))) + lam_init)
    scal = jnp.stack([lam, jnp.asarray(1.0 - lam_init, F32)]).astype(F32)
    o = _diff_attention(h.reshape(B, S, -1), scal, slopes, q_gain, k_gain, head_gain,
                        heads=heads, dk=dk, k_blk=AW // dv, v_blk=2 * AW // dv)
    o = o.reshape(T, AW)

    P = b_re.shape[3]
    L = min(S5_CHUNK, S)
    nc = S // L
    n_blk = SW // LANES
    assert B % SUBLANES == 0 and LANES % P == 0 and SW % LANES == 0
    m8, fwd_ops, bwd_ops = _s5_operators(lam_re, lam_im, log_dt, b_re, b_im, c_re, c_im, d_skip, L)
    u = h[:, u_off:u_off + SW].reshape(B, nc, L, n_blk, LANES)
    ut = u.transpose(3, 1, 0, 2, 4).reshape(n_blk, nc * B, L * LANES)
    yt = _s5_chunked(ut, m8, fwd_ops, bwd_ops, rows=B)
    y = yt.reshape(n_blk, nc, B, L, LANES).transpose(2, 1, 3, 0, 4).reshape(T, SW)

    merged = _glu_merge(y, o, h, w_glu.astype(BF16), w_attn_up.astype(BF16), w_ssm_up.astype(BF16),
                        ga_blk=ga_off // D, gb_blk=gb_off // D)
    return _outproj(merged, x2d, w_out.astype(BF16))


def kernel(x, norm1, w_in, q_gain, k_gain, lam_q1, lam_k1, lam_q2, lam_k2, head_gain, w_attn_up, ssm_lam_re, ssm_lam_im, ssm_log_dt, ssm_b_re, ssm_b_im, ssm_c_re, ssm_c_im, ssm_d, w_glu, w_ssm_up, w_out, norm2, w_router_group, b_router_group, w_router_expert, b_router_expert, w1, w3, w2):
    B, S, D = x.shape
    depth = norm1.shape[0]
    heads = w_attn_up.shape[1] // head_gain.shape[1]
    slopes = jnp.exp2(-8.0 * jnp.arange(1, heads + 1, dtype=F32) / heads)
    x2d = x.reshape(B * S, D).astype(F32)
    n_exp = w1.shape[1]
    w1b, w3b, w2b = (w.astype(BF16).reshape((depth * n_exp,) + w.shape[2:]) for w in (w1, w3, w2))
    for l in range(depth):
        x2d = _hybrid_mixer_residual(
            x2d, B, S, l, slopes, norm1[l], w_in[l], q_gain[l], k_gain[l], lam_q1[l], lam_k1[l],
            lam_q2[l], lam_k2[l], head_gain[l], w_attn_up[l], ssm_lam_re[l], ssm_lam_im[l],
            ssm_log_dt[l], ssm_b_re[l], ssm_b_im[l], ssm_c_re[l], ssm_c_im[l], ssm_d[l], w_glu[l],
            w_ssm_up[l], w_out[l])
        x2d = _hier_moe_residual(
            x2d, norm2[l], w_router_group[l], b_router_group[l], w_router_expert[l],
            b_router_expert[l], w1b, w3b, w2b, l * n_exp)
    return x2d.reshape(B, S, D).astype(x.dtype)
```

```python
import functools
import math

import jax
import jax.numpy as jnp
from jax import lax
from jax.experimental import pallas as pl
from jax.experimental.pallas import tpu as pltpu

F32 = jnp.float32
BF16 = jnp.bfloat16
RMS_EPS = 1e-6
LOG2E = math.log2(math.e)
LANES = 128
SUBLANES = 8
VMEM_LIMIT = 56 * 1024 * 1024
S5_CHUNK = 8
N_PAIRS = 6
_PAIR_A = (0, 0, 0, 1, 1, 2)
_PAIR_B = (1, 2, 3, 2, 3, 3)


def _cparams(sem):
    return pltpu.CompilerParams(dimension_semantics=sem, vmem_limit_bytes=VMEM_LIMIT)


def _sigmoid(x):
    return 1.0 / (1.0 + jnp.exp(-x))


def _pick_tile(n, target, unit=LANES):
    if n <= target:
        return n
    t = (target // unit) * unit
    while n % t:
        t -= unit
    return t


def _split_bf16(x):
    hi = x.astype(BF16)
    lo = (x - hi.astype(F32)).astype(BF16)
    return hi, lo


def _norm_inproj_kernel(x_ref, g_ref, w_ref, o_ref, xn_ref):
    @pl.when(pl.program_id(1) == 0)
    def _():
        x = x_ref[...]
        ms = jnp.mean(x * x, axis=-1, keepdims=True)
        xn_ref[...] = (x * lax.rsqrt(ms + RMS_EPS) * g_ref[...]).astype(BF16)

    o_ref[...] = jnp.dot(xn_ref[...], w_ref[...], preferred_element_type=F32).astype(o_ref.dtype)


def _norm_inproj(x2d, gain, w, *, tm=1024, tn=1024):
    T, D = x2d.shape
    N = w.shape[1]
    tm, tn = _pick_tile(T, tm, SUBLANES), _pick_tile(N, tn)
    return pl.pallas_call(
        _norm_inproj_kernel,
        grid=(T // tm, N // tn),
        in_specs=[pl.BlockSpec((tm, D), lambda i, j: (i, 0)),
                  pl.BlockSpec((1, D), lambda i, j: (0, 0)),
                  pl.BlockSpec((D, tn), lambda i, j: (0, j))],
        out_specs=pl.BlockSpec((tm, tn), lambda i, j: (i, j)),
        out_shape=jax.ShapeDtypeStruct((T, N), BF16),
        scratch_shapes=[pltpu.VMEM((tm, D), BF16)],
        compiler_params=_cparams(("parallel", "arbitrary")),
        name="norm_inproj",
    )(x2d, gain.reshape(1, D).astype(F32), w)


N_BIAS_LANES = 6
UNDERFLOW_LOG2 = 152.0
REACH_WINDOWS = (1, 2)


def _attn_kernel(scal_ref, slope_ref, reach_ref, q_ref, k_ref, v_ref, qg_ref, kg_ref, hg_ref, o_ref,
                 ka_ref, kb_ref, vt_ref, *tile_scratch, tq, tk, dk, n_sub):
    k_ref_map = (ka_ref, kb_ref)
    h = pl.program_id(1)
    i = pl.program_id(2)
    S = k_ref.shape[1]
    dv = 2 * dk
    nk = S // tk
    nt = (((1,), (1,)), ((), ()))
    lane = lax.broadcasted_iota(jnp.int32, (1, dv), 1)
    first = lane < dk
    feat_lane = jnp.where(first, lane, lane - dk)
    r = lax.broadcasted_iota(jnp.int32, (dv, dv), 0)
    c = lax.broadcasted_iota(jnp.int32, (dv, dv), 1)
    ones_bd = ((r < dk) == (c < dk)).astype(BF16)
    eye_dv = (r == c).astype(BF16)

    def comp_norm(x, gain):
        sq_hi, sq_lo = _split_bf16(x * x)
        ss = (jnp.dot(sq_hi, ones_bd, preferred_element_type=F32)
              + jnp.dot(sq_lo, ones_bd, preferred_element_type=F32))
        return x * lax.rsqrt(ss * (1.0 / dk) + RMS_EPS) * gain

    @pl.when(i == 0)
    def _():
        kn = comp_norm(k_ref[0].astype(F32), kg_ref[...])
        ds = lax.broadcasted_iota(jnp.int32, (S, dv), 0) & (tk - 1)
        ds_lo = ds & 255
        fl = jnp.broadcast_to(feat_lane, (S, dv))
        feat = jnp.where(fl < 3, ds_lo, jnp.where(fl < N_BIAS_LANES, ds - ds_lo, 0)).astype(F32)
        ka_ref[...] = jnp.where(first, kn, feat).astype(BF16)
        kb_ref[...] = jnp.where(first, feat, kn).astype(BF16)
        for jc in range(nk):
            vt_ref[jc, 0:dv, :] = lax.dot_general(
                eye_dv, v_ref[0, jc * tk:(jc + 1) * tk, :], nt,
                preferred_element_type=F32).astype(BF16)
            vt_ref[jc, dv:dv + 2 * SUBLANES, :] = jnp.ones((2 * SUBLANES, tk), BF16)

    slope2 = slope_ref[h] * LOG2E
    s_full = jnp.full((1, dv), slope2, F32)
    s_hi = s_full.astype(BF16).astype(F32)
    s_mid = (s_full - s_hi).astype(BF16).astype(F32)
    s_lo = (s_full - s_hi - s_mid).astype(BF16).astype(F32)
    q_feat = jnp.where((feat_lane == 0) | (feat_lane == 3), s_hi,
                       jnp.where((feat_lane == 1) | (feat_lane == 4), s_mid,
                                 jnp.where((feat_lane == 2) | (feat_lane == 5), s_lo, 0.0)))

    lam = scal_ref[0]
    out_scale = scal_ref[1]
    rq = lax.broadcasted_iota(jnp.int32, (tq, tq), 0)
    cq = lax.broadcasted_iota(jnp.int32, (tq, tq), 1)
    eye_q = (rq == cq).astype(BF16)
    hg_col = jnp.sum(jnp.where(r == c, jnp.broadcast_to(hg_ref[...], (dv, dv)), 0.0),
                     axis=1, keepdims=True) * out_scale

    m_sets = (tile_scratch[0:2], tile_scratch[2:4])
    acc_sets = (tile_scratch[4:6], tile_scratch[6:8])
    z_slots = (tile_scratch[8:10], tile_scratch[10:12])

    class Tile:
        def __init__(self, sub):
            self.sub = sub
            self.q0 = (i * n_sub + sub) * tq
            self.rows = pl.ds(sub * tq, tq)
            self.m_ref, self.acc_ref = m_sets[sub % 2], acc_sets[sub % 2]
            qn = comp_norm(q_ref[0, self.rows, :].astype(F32), qg_ref[...]) * (LOG2E * dk ** -0.5)
            self.q_plain = (jnp.where(first, qn, 0.0).astype(BF16),
                            jnp.where(first, 0.0, qn).astype(BF16))
            self.q_left = (jnp.where(first, qn, q_feat).astype(BF16),
                           jnp.where(first, q_feat, qn).astype(BF16))
            self.q_right = (jnp.where(first, qn, -q_feat).astype(BF16),
                            jnp.where(first, -q_feat, qn).astype(BF16))
            self.jd = self.q0 // tk
            self.t_row = (jnp.asarray(self.q0, F32)
                          + lax.broadcasted_iota(jnp.int32, (1, tq), 1).astype(F32))

        def chunk_of(self, step, window):
            if step == 0:
                return self.jd, None, None
            if window is None:
                j = (step - 1) + jnp.where((step - 1) >= self.jd, 1, 0)
                return j, j < self.jd, None
            k = (step + 1) // 2
            left = step % 2 == 1
            j = self.jd - k if left else self.jd + k
            valid = (j >= 0) & (j < nk)
            return jnp.clip(j, 0, nk - 1), left, valid

        def scores(self, step, window, z_ref):
            j, left, _ = self.chunk_of(step, window)
            start = pl.multiple_of(j * tk, tk)
            if left is None:
                key = lax.broadcasted_iota(jnp.int32, (tk, tq), 0) + j * tk
                qry = lax.broadcasted_iota(jnp.int32, (tk, tq), 1) + self.q0
                bias = (-slope2) * jnp.abs((qry - key).astype(F32))
            for c_idx in range(2):
                kc = k_ref_map[c_idx][pl.ds(start, tk), :]
                if left is None:
                    z = lax.dot_general(kc, self.q_plain[c_idx], nt,
                                        preferred_element_type=F32) + bias
                else:
                    if isinstance(left, bool):
                        qv = self.q_left[c_idx] if left else self.q_right[c_idx]
                    else:
                        qv = jnp.where(left, self.q_left[c_idx], self.q_right[c_idx])
                    z = lax.dot_general(kc, qv, nt, preferred_element_type=F32)
                z_ref[c_idx][...] = z

        def softmax_pv(self, step, window, z_ref):
            j, left, valid = self.chunk_of(step, window)
            vt = vt_ref[j]
            for c_idx in range(2):
                z = z_ref[c_idx][...]
                if left is None:
                    m_new = jnp.max(z, axis=0, keepdims=True)
                    p = jnp.exp2(z - m_new)
                    self.acc_ref[c_idx][...] = jnp.dot(vt, p.astype(BF16),
                                                       preferred_element_type=F32)
                else:
                    rterm = (jnp.where(left, -slope2, slope2)
                             * (self.t_row - jnp.asarray(j * tk, F32)))
                    if valid is not None:
                        rterm = jnp.where(valid, rterm, -jnp.inf)
                    m_prev = self.m_ref[c_idx][...]
                    m_new = jnp.maximum(m_prev, jnp.max(z, axis=0, keepdims=True) + rterm)
                    alpha = jnp.exp2(m_prev - m_new)
                    p = jnp.exp2(z - (m_new - rterm))
                    self.acc_ref[c_idx][...] = alpha * self.acc_ref[c_idx][...] + jnp.dot(
                        vt, p.astype(BF16), preferred_element_type=F32)
                self.m_ref[c_idx][...] = m_new

        def finish(self):
            acc = self.acc_ref
            o_t = (acc[0][0:dv, :] / acc[0][dv:dv + 1, :]
                   - lam * (acc[1][0:dv, :] / acc[1][dv:dv + 1, :]))
            ms = jnp.mean(o_t * o_t, axis=0, keepdims=True)
            on_t = (o_t * lax.rsqrt(ms + RMS_EPS) * hg_col).astype(BF16)
            o_ref[0, self.rows, :] = lax.dot_general(
                eye_q, on_t, nt, preferred_element_type=F32).astype(o_ref.dtype)

    def run(window):
        n_steps = nk if window is None else 2 * window + 1
        items = [(sub, step) for sub in range(n_sub) for step in range(n_steps)]
        tiles = {0: Tile(0)}
        tiles[0].scores(0, window, z_slots[0])
        pending_finish = None
        for n, (sub, step) in enumerate(items):
            if n + 2 < len(items) and items[n + 2][1] == 0:
                tiles[items[n + 2][0]] = Tile(items[n + 2][0])
            if n + 1 < len(items):
                sub_n, step_n = items[n + 1]
                if sub_n not in tiles:
                    tiles[sub_n] = Tile(sub_n)
                tiles[sub_n].scores(step_n, window, z_slots[(n + 1) % 2])
            tiles[sub].softmax_pv(step, window, z_slots[n % 2])
            if pending_finish is not None:
                pending_finish.finish()
                pending_finish = None
            if step == n_steps - 1:
                pending_finish = tiles[sub]
        pending_finish.finish()

    reach = reach_ref[h]
    windows = [w for w in REACH_WINDOWS if 2 * w + 1 < nk]
    for w in windows:
        pl.when(reach == w)(functools.partial(run, w))
    if windows:
        pl.when(functools.reduce(jnp.logical_and, [reach != w for w in windows]))(
            functools.partial(run, None))
    else:
        run(None)


def _diff_attention(h3, scal, slopes, q_gain, k_gain, head_gain, *, heads, dk, k_blk, v_blk,
                    tq=256, tk=512, n_sub=4):
    B, S, _ = h3.shape
    dv = 2 * dk
    tq, tk = min(tq, S), min(tk, S)
    n_sub = min(n_sub, S // tq)
    assert tk % tq == 0 and S % tk == 0
    assert S % (n_sub * tq) == 0
    qg = jnp.tile(q_gain.astype(F32), 2).reshape(1, dv)
    kg = jnp.tile(k_gain.astype(F32), 2).reshape(1, dv)
    hg = head_gain.astype(F32).reshape(1, dv)
    smem = pl.BlockSpec(memory_space=pltpu.SMEM)
    qk = (1.02 * LOG2E * math.sqrt(dk)) * jnp.max(jnp.abs(q_gain.astype(F32))) * jnp.max(
        jnp.abs(k_gain.astype(F32)))
    reach = jnp.zeros((heads,), jnp.int32)
    for w in sorted(REACH_WINDOWS, reverse=True):
        reach = jnp.where(2.0 * qk - slopes * (LOG2E * (w * tk + 1)) < -UNDERFLOW_LOG2, w, reach)
    rows_ext = dv + 2 * SUBLANES
    tile_scratch = ([pltpu.VMEM((1, tq), F32)] * 4
                    + [pltpu.VMEM((rows_ext, tq), F32)] * 4
                    + [pltpu.VMEM((tk, tq), F32)] * 4)
    return pl.pallas_call(
        functools.partial(_attn_kernel, tq=tq, tk=tk, dk=dk, n_sub=n_sub),
        grid=(B, heads, S // (n_sub * tq)),
        in_specs=[smem, smem, smem,
                  pl.BlockSpec((1, n_sub * tq, dv), lambda b, h, i: (b, i, h)),
                  pl.BlockSpec((1, S, dv), lambda b, h, i: (b, 0, k_blk + h)),
                  pl.BlockSpec((1, S, dv), lambda b, h, i: (b, 0, v_blk + h)),
                  pl.BlockSpec((1, dv), lambda b, h, i: (0, 0)),
                  pl.BlockSpec((1, dv), lambda b, h, i: (0, 0)),
                  pl.BlockSpec((1, dv), lambda b, h, i: (0, 0))],
        out_specs=pl.BlockSpec((1, n_sub * tq, dv), lambda b, h, i: (b, i, h)),
        out_shape=jax.ShapeDtypeStruct((B, S, heads * dv), BF16),
        scratch_shapes=[pltpu.VMEM((S, dv), BF16), pltpu.VMEM((S, dv), BF16),
                        pltpu.VMEM((S // tk, rows_ext, tk), BF16)] + tile_scratch,
        compiler_params=_cparams(("parallel", "parallel", "arbitrary")),
        name="diff_attn",
    )(scal, slopes, reach, h3, h3, h3, qg, kg, hg)


def _complex_powers(a_re, a_im, n):
    p_re = jnp.ones((1,) + a_re.shape, F32)
    p_im = jnp.zeros((1,) + a_re.shape, F32)
    s_re, s_im = a_re, a_im
    while p_re.shape[0] < n + 1:
        q_re = p_re * s_re - p_im * s_im
        q_im = p_re * s_im + p_im * s_re
        p_re = jnp.concatenate([p_re, q_re], axis=0)
        p_im = jnp.concatenate([p_im, q_im], axis=0)
        s_re, s_im = s_re * s_re - s_im * s_im, 2.0 * s_re * s_im
    return p_re[:n + 1], p_im[:n + 1]


def _s5_operators(lam_re, lam_im, log_dt, b_re, b_im, c_re, c_im, d_skip, L):
    hp = lax.Precision.HIGHEST
    G, N, P = b_re.shape[1], b_re.shape[2], b_re.shape[3]
    A = LANES // P
    O = G // A
    AN = A * N
    dt = jnp.exp(log_dt.astype(F32))[:, :, None]
    lr, li = lam_re.astype(F32), lam_im.astype(F32)
    mag = jnp.exp(lr * dt)
    a_re, a_im = mag * jnp.cos(li * dt), mag * jnp.sin(li * dt)
    nr = a_re - 1.0
    den = lr * lr + li * li
    f_re = (nr * lr + a_im * li) / den
    f_im = (a_im * lr - nr * li) / den
    br, bi = b_re.astype(F32), b_im.astype(F32)
    bb_re = f_re[..., None] * br - f_im[..., None] * bi
    bb_im = f_re[..., None] * bi + f_im[..., None] * br
    pw_re, pw_im = _complex_powers(a_re, a_im, L)
    cr, ci = c_re.astype(F32), c_im.astype(F32)
    cv_re = cr[None] * pw_re[:, :, :, None, :] - ci[None] * pw_im[:, :, :, None, :]
    cv_im = cr[None] * pw_im[:, :, :, None, :] + ci[None] * pw_re[:, :, :, None, :]
    ab_re = pw_re[:L, :, :, :, None] * bb_re[None] - pw_im[:L, :, :, :, None] * bb_im[None]
    ab_im = pw_re[:L, :, :, :, None] * bb_im[None] + pw_im[:L, :, :, :, None] * bb_re[None]
    kern = (jnp.einsum('tdgpn,dgnq->tdgpq', cv_re[:L], bb_re, precision=hp)
            - jnp.einsum('tdgpn,dgnq->tdgpq', cv_im[:L], bb_im, precision=hp))

    same_group = (jnp.arange(LANES)[:, None] // P) == (jnp.arange(LANES)[None, :] // P)

    dsk = d_skip.astype(F32).reshape(G, P)
    k0 = kern[0, 0] + kern[0, 1] + dsk[:, :, None] * jnp.eye(P, dtype=F32)[None]
    kfull = jnp.concatenate([kern[1:, 1][::-1], k0[None], kern[1:, 0]], axis=0)
    kq = kfull.reshape(2 * L - 1, O, A, P, P).transpose(0, 1, 2, 4, 3)
    kq = kq.reshape(2 * L - 1, O, LANES, P)
    tiles = jnp.where(same_group, jnp.tile(kq, (1, 1, 1, A)), 0.0).astype(BF16)
    idx = (jnp.arange(L)[None, :] - jnp.arange(L)[:, None]) + (L - 1)
    m8 = tiles[idx].transpose(2, 0, 3, 1, 4).reshape(O, L * LANES, L * LANES)

    state_group = (jnp.arange(LANES)[:, None] // P) == (jnp.arange(AN)[None, :] // N)

    def st_tiles(ab):
        x = ab.reshape(L, O, A, N, P).transpose(0, 1, 2, 4, 3).reshape(L, O, LANES, N)
        x = jnp.where(state_group, jnp.tile(x, (1, 1, 1, A)), 0.0)
        return x.transpose(1, 0, 2, 3).reshape(O, L * LANES, AN)

    def out_tiles(cv):
        x = cv.reshape(L, O, A, P, N).transpose(0, 1, 2, 4, 3).reshape(L, O, AN, P)
        x = jnp.where(state_group.T, jnp.tile(x, (1, 1, 1, A)), 0.0)
        return x.transpose(1, 2, 0, 3).reshape(O, AN, L * LANES)

    ops = []
    for d in range(2):
        order = slice(None, None, -1) if d == 0 else slice(None)
        taus = slice(1, L + 1) if d == 0 else slice(L, 0, -1)
        wst = jnp.concatenate([st_tiles(ab_re[order, d]), st_tiles(ab_im[order, d])],
                              axis=2).astype(BF16)
        wout = jnp.concatenate([out_tiles(cv_re[taus, d]), -out_tiles(cv_im[taus, d])],
                               axis=1).astype(BF16)
        apow = jnp.stack([pw_re[L, d].reshape(O, AN), pw_im[L, d].reshape(O, AN)]
                         + [jnp.zeros((O, AN), F32)] * (SUBLANES - 2), axis=1)
        ops.append((wst, wout, apow))
    return m8, ops[0], ops[1]


def _gelu_tanh(y):
    c = math.sqrt(2.0 / math.pi)
    return 0.5 * y * (1.0 + jnp.tanh(c * (y + 0.044715 * (y * y * y))))


def _s5_pass_kernel(*refs, rows, n_chunks, reverse, with_toeplitz):
    if with_toeplitz:
        x_ref, m_ref, wst_ref, wout_ref, ap_ref, y_ref, h_re_sc, h_im_sc, s_sc, p_sc = refs
    else:
        x_ref, wst_ref, wout_ref, ap_ref, part_ref, y_ref, h_re_sc, h_im_sc, s_sc, p_sc = refs
    n = h_re_sc.shape[1]

    @pl.when(pl.program_id(1) == 0)
    def _():
        h_re_sc[...] = jnp.zeros(h_re_sc.shape, F32)
        h_im_sc[...] = jnp.zeros(h_im_sc.shape, F32)

    x = x_ref[0]
    s_sc[...] = jnp.dot(x, wst_ref[0], preferred_element_type=F32)
    ar = ap_ref[0, 0:1, :]
    ai = ap_ref[0, 1:2, :]

    def step(k, carry):
        h_re, h_im = carry
        c = (n_chunks - 1 - k) if reverse else k
        r = pl.ds(pl.multiple_of(c * rows, rows), rows)
        p_sc[r, 0:n] = h_re
        p_sc[r, n:2 * n] = h_im
        return (ar * h_re - ai * h_im + s_sc[r, 0:n], ar * h_im + ai * h_re + s_sc[r, n:2 * n])

    h_re, h_im = lax.fori_loop(0, n_chunks, step, (h_re_sc[...], h_im_sc[...]))
    h_re_sc[...] = h_re
    h_im_sc[...] = h_im

    y = jnp.dot(p_sc[...].astype(BF16), wout_ref[0], preferred_element_type=F32)
    if with_toeplitz:
        y_ref[0] = y + jnp.dot(x, m_ref[0], preferred_element_type=F32)
    else:
        y_ref[0] = _gelu_tanh(y + part_ref[0]).astype(y_ref.dtype)


def _s5_chunked(x, m8, fwd_ops, bwd_ops, *, rows, tr=512):
    O, R, W = x.shape
    n2 = fwd_ops[0].shape[2]
    tr = min(tr, R)
    nt = R // tr
    scratch = [pltpu.VMEM((rows, n2 // 2), F32), pltpu.VMEM((rows, n2 // 2), F32),
               pltpu.VMEM((tr, n2), F32), pltpu.VMEM((tr, n2), F32)]

    def row_block(order):
        return pl.BlockSpec((1, tr, W), lambda o, r: (o, order(r), 0))

    def per_block(shape):
        return pl.BlockSpec((1,) + shape, lambda o, r: (o, 0, 0))

    wst, wout, apow = fwd_ops
    part = pl.pallas_call(
        functools.partial(_s5_pass_kernel, rows=rows, n_chunks=tr // rows, reverse=False,
                          with_toeplitz=True),
        grid=(O, nt),
        in_specs=[row_block(lambda r: r), per_block((W, W)), per_block((W, n2)),
                  per_block((n2, W)), per_block((SUBLANES, n2 // 2))],
        out_specs=row_block(lambda r: r),
        out_shape=jax.ShapeDtypeStruct((O, R, W), F32),
        scratch_shapes=scratch,
        compiler_params=_cparams(("parallel", "arbitrary")),
        name="s5_forward",
    )(x, m8, wst, wout, apow)
    wst, wout, apow = bwd_ops
    return pl.pallas_call(
        functools.partial(_s5_pass_kernel, rows=rows, n_chunks=tr // rows, reverse=True,
                          with_toeplitz=False),
        grid=(O, nt),
        in_specs=[row_block(lambda r: nt - 1 - r), per_block((W, n2)), per_block((n2, W)),
                  per_block((SUBLANES, n2 // 2)), row_block(lambda r: nt - 1 - r)],
        out_specs=row_block(lambda r: nt - 1 - r),
        out_shape=jax.ShapeDtypeStruct((O, R, W), BF16),
        scratch_shapes=scratch,
        compiler_params=_cparams(("parallel", "arbitrary")),
        name="s5_backward",
    )(x, wst, wout, apow, part)


def _glu_merge_kernel(y_ref, o_ref, ga_ref, gb_ref, wg_ref, wa_ref, wb_ref, out_ref):
    y = y_ref[...]
    z = jnp.dot(y, wg_ref[...], preferred_element_type=F32)
    s = (y.astype(F32) * _sigmoid(z)).astype(BF16)
    a_out = jnp.dot(o_ref[...], wa_ref[...], preferred_element_type=F32)
    b_out = jnp.dot(s, wb_ref[...], preferred_element_type=F32)
    merged = (_sigmoid(ga_ref[...].astype(F32)) * a_out + _sigmoid(gb_ref[...].astype(F32)) * b_out)
    out_ref[...] = merged.astype(out_ref.dtype)


def _glu_merge(y, o, h, wg, wa, wb, *, ga_blk, gb_blk, tm=256):
    T, SW = y.shape
    AW = o.shape[1]
    D = wa.shape[1]
    tm = min(tm, T)
    return pl.pallas_call(
        _glu_merge_kernel,
        grid=(T // tm,),
        in_specs=[pl.BlockSpec((tm, SW), lambda i: (i, 0)),
                  pl.BlockSpec((tm, AW), lambda i: (i, 0)),
                  pl.BlockSpec((tm, D), lambda i: (i, ga_blk)),
                  pl.BlockSpec((tm, D), lambda i: (i, gb_blk)),
                  pl.BlockSpec((SW, SW), lambda i: (0, 0)),
                  pl.BlockSpec((AW, D), lambda i: (0, 0)),
                  pl.BlockSpec((SW, D), lambda i: (0, 0))],
        out_specs=pl.BlockSpec((tm, D), lambda i: (i, 0)),
        out_shape=jax.ShapeDtypeStruct((T, D), BF16),
        compiler_params=_cparams(("parallel",)),
        name="glu_merge",
    )(y, o, h, h, wg, wa, wb)


def _outproj_kernel(m_ref, x_ref, w_ref, o_ref):
    o_ref[...] = x_ref[...] + jnp.dot(m_ref[...], w_ref[...], preferred_element_type=F32)


def _outproj(merged, x2d, w, *, tm=512):
    T, D = x2d.shape
    tm = min(tm, T)
    return pl.pallas_call(
        _outproj_kernel,
        grid=(T // tm,),
        in_specs=[pl.BlockSpec((tm, D), lambda i: (i, 0)),
                  pl.BlockSpec((tm, D), lambda i: (i, 0)),
                  pl.BlockSpec((D, D), lambda i: (0, 0))],
        out_specs=pl.BlockSpec((tm, D), lambda i: (i, 0)),
        out_shape=jax.ShapeDtypeStruct((T, D), F32),
        compiler_params=_cparams(("parallel",)),
        name="outproj",
    )(merged, x2d, w)


def _router_kernel(x_ref, g_ref, whi_ref, wlo_ref, b_ref, xe_ref, meta_ref, cnt_ref,
                   tri_sc, eye_sc, carry_sc, *, n_groups, per_group, e_row0):
    t = pl.program_id(0)
    tm, D = x_ref.shape
    n_rows = whi_ref.shape[0]

    @pl.when(t == 0)
    def _():
        r = lax.broadcasted_iota(jnp.int32, (tm, tm), 0)
        c = lax.broadcasted_iota(jnp.int32, (tm, tm), 1)
        tri_sc[...] = (r < c).astype(BF16)
        eye_sc[...] = (r == c).astype(BF16)
        carry_sc[...] = jnp.zeros(carry_sc.shape, F32)

    x = x_ref[...]
    ms = jnp.mean(x * x, axis=-1, keepdims=True)
    xn = x * lax.rsqrt(ms + RMS_EPS) * g_ref[...]
    xe_ref[:, 0:D] = xn

    nt = (((1,), (1,)), ((), ()))
    x_hi, x_lo = _split_bf16(xn)
    whi, wlo = whi_ref[...], wlo_ref[...]
    logits = (lax.dot_general(whi, x_hi, nt, preferred_element_type=F32)
              + lax.dot_general(whi, x_lo, nt, preferred_element_type=F32)
              + lax.dot_general(wlo, x_hi, nt, preferred_element_type=F32)) + b_ref[...]

    g = [logits[k:k + 1, :] for k in range(n_groups)]
    g_max = functools.reduce(jnp.maximum, g)
    g_idx = jnp.full(g_max.shape, n_groups - 1, jnp.int32)
    for k in range(n_groups - 2, -1, -1):
        g_idx = jnp.where(g[k] == g_max, k, g_idx)
    g_w = 1.0 / functools.reduce(lambda a, b: a + b, [jnp.exp(v - g_max) for v in g])

    e_in = []
    for j in range(per_group):
        v = logits[e_row0 + j:e_row0 + j + 1, :]
        for k in range(1, n_groups):
            row = e_row0 + k * per_group + j
            v = jnp.where(g_idx == k, logits[row:row + 1, :], v)
        e_in.append(v)

    def first_argmax(vals):
        vmax = functools.reduce(jnp.maximum, vals)
        idx = jnp.full(vmax.shape, len(vals) - 1, jnp.int32)
        for k in range(len(vals) - 2, -1, -1):
            idx = jnp.where(vals[k] == vmax, k, idx)
        return vmax, idx

    v1, i1 = first_argmax(e_in)
    rest = [jnp.where(i1 == j, -jnp.inf, e_in[j]) for j in range(per_group)]
    v2, i2 = first_argmax(rest)
    e2 = jnp.exp(v2 - v1)
    w1 = g_w / (1.0 + e2)
    w2 = g_w * e2 / (1.0 + e2)
    lo_first = i1 < i2
    a = jnp.where(lo_first, i1, i2)
    b = jnp.where(lo_first, i2, i1)
    w_a = jnp.where(lo_first, w1, w2)
    w_b = jnp.where(lo_first, w2, w1)
    pair = jnp.where(a == 0, b - 1, jnp.where(a == 1, b + 1, 5))
    bucket = g_idx * N_PAIRS + pair

    rows = lax.broadcasted_iota(jnp.int32, (n_rows, tm), 0)
    onehot = (rows == bucket).astype(F32)
    prefix = jnp.dot(onehot.astype(BF16), tri_sc[...], preferred_element_type=F32)
    rank = jnp.sum(onehot * (prefix + carry_sc[...]), axis=0, keepdims=True)
    carry_sc[...] = carry_sc[...] + jnp.sum(onehot, axis=1, keepdims=True)
    cnt_ref[...] = jnp.broadcast_to(carry_sc[...], cnt_ref.shape)

    meta_ref[0, 0:1, :] = bucket.astype(F32)
    meta_ref[0, 1:2, :] = rank
    meta_ref[0, 2:3, :] = w_a
    meta_ref[0, 3:4, :] = w_b
    meta_ref[0, 4:8, :] = jnp.zeros((4, tm), F32)

    eye = eye_sc[...]
    for blk, w in enumerate((w_a, w_b)):
        wrep = jnp.broadcast_to(w, (LANES, tm))
        hi, lo = _split_bf16(wrep)
        col = (lax.dot_general(eye, hi, nt, preferred_element_type=F32)
               + lax.dot_general(eye, lo, nt, preferred_element_type=F32))
        xe_ref[:, D + blk * LANES:D + (blk + 1) * LANES] = col


def _router(x2d, gain, w_rg, b_rg, w_re, b_re, *, tm=512):
    T, D = x2d.shape
    n_groups = w_rg.shape[1]
    n_exp = w_re.shape[1]
    per_group = n_exp // n_groups
    e_row0 = SUBLANES
    n_rows = e_row0 + n_exp
    n_rows = ((n_rows + SUBLANES - 1) // SUBLANES) * SUBLANES
    assert n_groups * N_PAIRS <= n_rows and per_group == 4
    wt = jnp.zeros((n_rows, D), F32)
    wt = wt.at[0:n_groups].set(w_rg.astype(F32).T).at[e_row0:e_row0 + n_exp].set(w_re.astype(F32).T)
    bias = jnp.zeros((n_rows, 1), F32)
    bias = bias.at[0:n_groups, 0].set(b_rg.astype(F32)).at[e_row0:e_row0 + n_exp, 0].set(b_re.astype(F32))
    w_hi, w_lo = _split_bf16(wt)
    tm = min(tm, T)
    De = D + 2 * LANES
    return pl.pallas_call(
        functools.partial(_router_kernel, n_groups=n_groups, per_group=per_group, e_row0=e_row0),
        grid=(T // tm,),
        in_specs=[pl.BlockSpec((tm, D), lambda t: (t, 0)),
                  pl.BlockSpec((1, D), lambda t: (0, 0)),
                  pl.BlockSpec((n_rows, D), lambda t: (0, 0)),
                  pl.BlockSpec((n_rows, D), lambda t: (0, 0)),
                  pl.BlockSpec((n_rows, 1), lambda t: (0, 0))],
        out_specs=[pl.BlockSpec((tm, De), lambda t: (t, 0)),
                   pl.BlockSpec((1, SUBLANES, tm), lambda t: (t, 0, 0)),
                   pl.BlockSpec((n_rows, LANES), lambda t: (0, 0))],
        out_shape=[jax.ShapeDtypeStruct((T, De), F32),
                   jax.ShapeDtypeStruct((T // tm, SUBLANES, tm), F32),
                   jax.ShapeDtypeStruct((n_rows, LANES), F32)],
        scratch_shapes=[pltpu.VMEM((tm, tm), BF16), pltpu.VMEM((tm, tm), BF16),
                        pltpu.VMEM((n_rows, 1), F32)],
        compiler_params=_cparams(("arbitrary",)),
        name="router",
    )(x2d, gain.reshape(1, D).astype(F32), w_hi, w_lo, bias)


def _dispatch_kernel(pos_ref, x_ref, xs_in_ref, xs_ref, sem, *, tm):
    del xs_in_ref
    base = pl.program_id(0) * tm

    def issue(r2, carry):
        for lane in range(2):
            r = 2 * r2 + lane
            pltpu.make_async_copy(x_ref.at[pl.ds(r, 1)], xs_ref.at[pl.ds(pos_ref[base + r], 1)],
                                  sem).start(priority=lane)
        return carry

    lax.fori_loop(0, tm // 2, issue, 0, unroll=4)

    def drain(r, carry):
        pltpu.make_async_copy(x_ref.at[pl.ds(0, 1)], xs_ref.at[pl.ds(0, 1)], sem).wait()
        return carry

    lax.fori_loop(0, tm, drain, 0, unroll=8)


def _dispatch(pos, xe, n_rows, *, tm=512):
    T, De = xe.shape
    tm = min(tm, T)
    xs0 = jnp.zeros((n_rows, De), F32)
    return pl.pallas_call(
        functools.partial(_dispatch_kernel, tm=tm),
        grid_spec=pltpu.PrefetchScalarGridSpec(
            num_scalar_prefetch=1,
            grid=(T // tm,),
            in_specs=[pl.BlockSpec((tm, De), lambda t, pos: (t, 0)),
                      pl.BlockSpec(memory_space=pl.ANY)],
            out_specs=pl.BlockSpec(memory_space=pl.ANY),
            scratch_shapes=[pltpu.SemaphoreType.DMA(())]),
        out_shape=jax.ShapeDtypeStruct((n_rows, De), F32),
        input_output_aliases={2: 0},
        compiler_params=_cparams(("arbitrary",)),
        name="moe_dispatch",
    )(pos, xe, xs0)


def _collect_kernel(pos_ref, ys_ref, x_ref, o_ref, buf, sem, *, tm):
    base = pl.program_id(0) * tm

    def issue(r2, carry):
        for lane in range(2):
            r = 2 * r2 + lane
            pltpu.make_async_copy(ys_ref.at[pl.ds(pos_ref[base + r], 1)], buf.at[pl.ds(r, 1)],
                                  sem).start(priority=lane)
        return carry

    lax.fori_loop(0, tm // 2, issue, 0, unroll=4)

    def drain(r, carry):
        pltpu.make_async_copy(ys_ref.at[pl.ds(0, 1)], buf.at[pl.ds(0, 1)], sem).wait()
        return carry

    lax.fori_loop(0, tm, drain, 0, unroll=8)
    o_ref[...] = x_ref[...] + buf[...]


def _collect(pos, ys, x2d, *, tm=512):
    T, D = x2d.shape
    tm = min(tm, T)
    return pl.pallas_call(
        functools.partial(_collect_kernel, tm=tm),
        grid_spec=pltpu.PrefetchScalarGridSpec(
            num_scalar_prefetch=1,
            grid=(T // tm,),
            in_specs=[pl.BlockSpec(memory_space=pl.ANY),
                      pl.BlockSpec((tm, D), lambda t, pos: (t, 0))],
            out_specs=pl.BlockSpec((tm, D), lambda t, pos: (t, 0)),
            scratch_shapes=[pltpu.VMEM((tm, D), F32), pltpu.SemaphoreType.DMA(())]),
        out_shape=jax.ShapeDtypeStruct((T, D), F32),
        compiler_params=_cparams(("arbitrary",)),
        name="moe_collect",
    )(pos, ys, x2d)


def _ffn_kernel(te_ref, tv_ref, x_ref, w1_ref, w3_ref, w2_ref, o_ref, xb_sc, *, D):
    i, j, f = pl.program_id(0), pl.program_id(1), pl.program_id(2)
    first = jnp.logical_and(j == 0, f == 0)

    @pl.when(first)
    def _():
        xb_sc[...] = x_ref[:, 0:D].astype(BF16)
        o_ref[...] = jnp.zeros(o_ref.shape, F32)

    @pl.when(tv_ref[i] != 0)
    def _():
        xb = xb_sc[...]
        h1 = jnp.dot(xb, w1_ref[0], preferred_element_type=F32)
        h3 = jnp.dot(xb, w3_ref[0], preferred_element_type=F32)
        hid = (h1 * _sigmoid(h1) * h3).astype(BF16)
        y = jnp.dot(hid, w2_ref[0], preferred_element_type=F32)
        w = jnp.where(j == 0, x_ref[:, D:D + 1], x_ref[:, D + LANES:D + LANES + 1])
        o_ref[...] += w * y


def _expert_ffn(tile_expert, tile_valid, xs, w1, w3, w2, *, tm, tf=512):
    R, De = xs.shape
    E, D, FF = w1.shape
    tf = min(tf, FF)
    n_tiles = R // tm
    return pl.pallas_call(
        functools.partial(_ffn_kernel, D=D),
        grid_spec=pltpu.PrefetchScalarGridSpec(
            num_scalar_prefetch=2,
            grid=(n_tiles, 2, FF // tf),
            in_specs=[pl.BlockSpec((tm, De), lambda i, j, f, te, tv: (i, 0)),
                      pl.BlockSpec((1, D, tf), lambda i, j, f, te, tv: (te[2 * i + j], 0, f)),
                      pl.BlockSpec((1, D, tf), lambda i, j, f, te, tv: (te[2 * i + j], 0, f)),
                      pl.BlockSpec((1, tf, D), lambda i, j, f, te, tv: (te[2 * i + j], f, 0))],
            out_specs=pl.BlockSpec((tm, D), lambda i, j, f, te, tv: (i, 0)),
            scratch_shapes=[pltpu.VMEM((tm, D), BF16)]),
        out_shape=jax.ShapeDtypeStruct((R, D), F32),
        compiler_params=_cparams(("parallel", "arbitrary", "arbitrary")),
        name="expert_ffn",
    )(tile_expert, tile_valid, xs, w1, w3, w2)


def _moe_plan(meta, counts, n_groups, per_group, T, tm_e):
    n_buckets = n_groups * N_PAIRS
    bucket = meta[:, 0, :].reshape(T).astype(jnp.int32)
    rank = meta[:, 1, :].reshape(T).astype(jnp.int32)
    cnt = counts[:n_buckets, 0].astype(jnp.int32)
    padded = ((cnt + tm_e - 1) // tm_e) * tm_e
    ends = jnp.cumsum(padded)
    offs = ends - padded
    pos = offs[bucket] + rank
    n_tiles = T // tm_e + n_buckets
    starts = jnp.arange(n_tiles, dtype=jnp.int32) * tm_e
    tile_valid = (starts < ends[-1]).astype(jnp.int32)
    last_start = jnp.maximum(ends[-1] - tm_e, 0)
    tb = jnp.sum((ends[None, :] <= jnp.minimum(starts, last_start)[:, None]).astype(jnp.int32), axis=1)
    tb = jnp.minimum(tb, n_buckets - 1)
    grp, pair = tb // N_PAIRS, tb % N_PAIRS
    ea = grp * per_group + jnp.asarray(_PAIR_A, jnp.int32)[pair]
    eb = grp * per_group + jnp.asarray(_PAIR_B, jnp.int32)[pair]
    tile_expert = jnp.stack([ea, eb], axis=1).reshape(-1)
    return pos, tile_expert, tile_valid, n_tiles


def _hier_moe_residual(x2d, gain, w_rg, b_rg, w_re, b_re, w1, w3, w2, expert0, *, tm_e=512):
    T, D = x2d.shape
    n_groups = w_rg.shape[1]
    per_group = w_re.shape[1] // n_groups
    tm_e = min(tm_e, T)
    xe, meta, counts = _router(x2d, gain, w_rg, b_rg, w_re, b_re)
    pos, tile_expert, tile_valid, n_tiles = _moe_plan(meta, counts, n_groups, per_group, T, tm_e)
    xs = _dispatch(pos, xe, n_tiles * tm_e)
    ys = _expert_ffn(tile_expert + expert0, tile_valid, xs, w1, w3, w2, tm=tm_e)
    return _collect(pos, ys, x2d)


def _hybrid_mixer_residual(x2d, B, S, layer_idx, slopes, norm1, w_in, q_gain, k_gain, lam_q1, lam_k1,
                           lam_q2, lam_k2, head_gain, w_attn_up, lam_re, lam_im, log_dt, b_re, b_im,
                           c_re, c_im, d_skip, w_glu, w_ssm_up, w_out):
    T, D = x2d.shape
    dk = q_gain.shape[0]
    dv = head_gain.shape[0]
    AW, SW = w_attn_up.shape[0], w_glu.shape[0]
    heads = AW // dv
    assert dv == 2 * dk == LANES
    u_off = 3 * AW
    ga_off = u_off + SW
    gb_off = ga_off + D
    assert ga_off % D == 0

    h = _norm_inproj(x2d, norm1, w_in.astype(BF16))

    lam_init = 0.8 - 0.6 * math.exp(-0.3 * layer_idx)
    lam = (jnp.exp(jnp.sum(lam_q1.astype(F32) * lam_k1.astype(F32)))
           - jnp.exp(jnp.sum(lam_q2.astype(F32) * lam_k2.astype(F32))) + lam_init)
    scal = jnp.stack([lam, jnp.asarray(1.0 - lam_init, F32)]).astype(F32)
    o = _diff_attention(h.reshape(B, S, -1), scal, slopes, q_gain, k_gain, head_gain,
                        heads=heads, dk=dk, k_blk=AW // dv, v_blk=2 * AW // dv)
    o = o.reshape(T, AW)

    P = b_re.shape[3]
    L = min(S5_CHUNK, S)
    nc = S // L
    n_blk = SW // LANES
    assert B % SUBLANES == 0 and LANES % P == 0 and SW % LANES == 0
    m8, fwd_ops, bwd_ops = _s5_operators(lam_re, lam_im, log_dt, b_re, b_im, c_re, c_im, d_skip, L)
    u = h[:, u_off:u_off + SW].reshape(B, nc, L, n_blk, LANES)
    ut = u.transpose(3, 1, 0, 2, 4).reshape(n_blk, nc * B, L * LANES)
    yt = _s5_chunked(ut, m8, fwd_ops, bwd_ops, rows=B)
    y = yt.reshape(n_blk, nc, B, L, LANES).transpose(2, 1, 3, 0, 4).reshape(T, SW)

    merged = _glu_merge(y, o, h, w_glu.astype(BF16), w_attn_up.astype(BF16), w_ssm_up.astype(BF16),
                        ga_blk=ga_off // D, gb_blk=gb_off // D)
    return _outproj(merged, x2d, w_out.astype(BF16))


def kernel(x, norm1, w_in, q_gain, k_gain, lam_q1, lam_k1, lam_q2, lam_k2, head_gain, w_attn_up, ssm_lam_re, ssm_lam_im, ssm_log_dt, ssm_b_re, ssm_b_im, ssm_c_re, ssm_c_im, ssm_d, w_glu, w_ssm_up, w_out, norm2, w_router_group, b_router_group, w_router_expert, b_router_expert, w1, w3, w2):
    B, S, D = x.shape
    depth = norm1.shape[0]
    heads = w_attn_up.shape[1] // head_gain.shape[1]
    slopes = jnp.exp2(-8.0 * jnp.arange(1, heads + 1, dtype=F32) / heads)
    x2d = x.reshape(B * S, D).astype(F32)
    n_exp = w1.shape[1]
    w1b, w3b, w2b = (w.astype(BF16).reshape((depth * n_exp,) + w.shape[2:]) for w in (w1, w3, w2))
    for l in range(depth):
        x2d = _hybrid_mixer_residual(
            x2d, B, S, l, slopes, norm1[l], w_in[l], q_gain[l], k_gain[l], lam_q1[l], lam_k1[l],
            lam_q2[l], lam_k2[l], head_gain[l], w_attn_up[l], ssm_lam_re[l], ssm_lam_im[l],
            ssm_log_dt[l], ssm_b_re[l], ssm_b_im[l], ssm_c_re[l], ssm_c_im[l], ssm_d[l], w_glu[l],
            w_ssm_up[l], w_out[l])
        x2d = _hier_moe_residual(
            x2d, norm2[l], w_router_group[l], b_router_group[l], w_router_expert[l],
            b_router_expert[l], w1b, w3b, w2b, l * n_exp)
    return x2d.reshape(B, S, D).astype(x.dtype)
```

```python
import functools
import math

import jax
import jax.numpy as jnp
from jax import lax
from jax.experimental import pallas as pl
from jax.experimental.pallas import tpu as pltpu

F32 = jnp.float32
BF16 = jnp.bfloat16
RMS_EPS = 1e-6
LOG2E = math.log2(math.e)
LANES = 128
SUBLANES = 8
VMEM_LIMIT = 56 * 1024 * 1024
S5_CHUNK = 8
N_PAIRS = 6
_PAIR_A = (0, 0, 0, 1, 1, 2)
_PAIR_B = (1, 2, 3, 2, 3, 3)


def _cparams(sem):
    return pltpu.CompilerParams(dimension_semantics=sem, vmem_limit_bytes=VMEM_LIMIT)


def _sigmoid(x):
    return 1.0 / (1.0 + jnp.exp(-x))


def _pick_tile(n, target, unit=LANES):
    if n <= target:
        return n
    t = (target // unit) * unit
    while n % t:
        t -= unit
    return t


def _split_bf16(x):
    hi = x.astype(BF16)
    lo = (x - hi.astype(F32)).astype(BF16)
    return hi, lo


def _norm_inproj_kernel(x_ref, g_ref, w_ref, o_ref, xn_ref):
    @pl.when(pl.program_id(1) == 0)
    def _():
        x = x_ref[...]
        ms = jnp.mean(x * x, axis=-1, keepdims=True)
        xn_ref[...] = (x * lax.rsqrt(ms + RMS_EPS) * g_ref[...]).astype(BF16)

    o_ref[...] = jnp.dot(xn_ref[...], w_ref[...], preferred_element_type=F32).astype(o_ref.dtype)


def _norm_inproj(x2d, gain, w, *, tm=1024, tn=1024):
    T, D = x2d.shape
    N = w.shape[1]
    tm, tn = _pick_tile(T, tm, SUBLANES), _pick_tile(N, tn)
    return pl.pallas_call(
        _norm_inproj_kernel,
        grid=(T // tm, N // tn),
        in_specs=[pl.BlockSpec((tm, D), lambda i, j: (i, 0)),
                  pl.BlockSpec((1, D), lambda i, j: (0, 0)),
                  pl.BlockSpec((D, tn), lambda i, j: (0, j))],
        out_specs=pl.BlockSpec((tm, tn), lambda i, j: (i, j)),
        out_shape=jax.ShapeDtypeStruct((T, N), BF16),
        scratch_shapes=[pltpu.VMEM((tm, D), BF16)],
        compiler_params=_cparams(("parallel", "arbitrary")),
        name="norm_inproj",
    )(x2d, gain.reshape(1, D).astype(F32), w)


N_BIAS_LANES = 6
UNDERFLOW_LOG2 = 152.0
REACH_WINDOWS = (1, 2)


def _attn_kernel(scal_ref, slope_ref, reach_ref, q_ref, k_ref, v_ref, qg_ref, kg_ref, hg_ref, o_ref,
                 ka_ref, kb_ref, vt_ref, *tile_scratch, tq, tk, dk, n_sub):
    k_ref_map = (ka_ref, kb_ref)
    h = pl.program_id(1)
    i = pl.program_id(2)
    S = k_ref.shape[1]
    dv = 2 * dk
    nk = S // tk
    nt = (((1,), (1,)), ((), ()))
    lane = lax.broadcasted_iota(jnp.int32, (1, dv), 1)
    first = lane < dk
    feat_lane = jnp.where(first, lane, lane - dk)
    r = lax.broadcasted_iota(jnp.int32, (dv, dv), 0)
    c = lax.broadcasted_iota(jnp.int32, (dv, dv), 1)
    ones_bd = ((r < dk) == (c < dk)).astype(BF16)
    eye_dv = (r == c).astype(BF16)

    def comp_norm(x, gain):
        sq_hi, sq_lo = _split_bf16(x * x)
        ss = (jnp.dot(sq_hi, ones_bd, preferred_element_type=F32)
              + jnp.dot(sq_lo, ones_bd, preferred_element_type=F32))
        return x * lax.rsqrt(ss * (1.0 / dk) + RMS_EPS) * gain

    @pl.when(i == 0)
    def _():
        kn = comp_norm(k_ref[0].astype(F32), kg_ref[...])
        ds = lax.broadcasted_iota(jnp.int32, (S, dv), 0) & (tk - 1)
        ds_lo = ds & 255
        fl = jnp.broadcast_to(feat_lane, (S, dv))
        feat = jnp.where(fl < 3, ds_lo, jnp.where(fl < N_BIAS_LANES, ds - ds_lo, 0)).astype(F32)
        ka_ref[...] = jnp.where(first, kn, feat).astype(BF16)
        kb_ref[...] = jnp.where(first, feat, kn).astype(BF16)
        for jc in range(nk):
            vt_ref[jc, 0:dv, :] = lax.dot_general(
                eye_dv, v_ref[0, jc * tk:(jc + 1) * tk, :], nt,
                preferred_element_type=F32).astype(BF16)
            vt_ref[jc, dv:dv + 2 * SUBLANES, :] = jnp.ones((2 * SUBLANES, tk), BF16)

    slope2 = slope_ref[h] * LOG2E
    s_full = jnp.full((1, dv), slope2, F32)
    s_hi = s_full.astype(BF16).astype(F32)
    s_mid = (s_full - s_hi).astype(BF16).astype(F32)
    s_lo = (s_full - s_hi - s_mid).astype(BF16).astype(F32)
    q_feat = jnp.where((feat_lane == 0) | (feat_lane == 3), s_hi,
                       jnp.where((feat_lane == 1) | (feat_lane == 4), s_mid,
                                 jnp.where((feat_lane == 2) | (feat_lane == 5), s_lo, 0.0)))

    lam = scal_ref[0]
    out_scale = scal_ref[1]
    rq = lax.broadcasted_iota(jnp.int32, (tq, tq), 0)
    cq = lax.broadcasted_iota(jnp.int32, (tq, tq), 1)
    eye_q = (rq == cq).astype(BF16)
    hg_col = jnp.sum(jnp.where(r == c, jnp.broadcast_to(hg_ref[...], (dv, dv)), 0.0),
                     axis=1, keepdims=True) * out_scale

    m_sets = (tile_scratch[0:2], tile_scratch[2:4])
    acc_sets = (tile_scratch[4:6], tile_scratch[6:8])
    z_slots = (tile_scratch[8:10], tile_scratch[10:12])

    class Tile:
        def __init__(self, sub):
            self.sub = sub
            self.q0 = (i * n_sub + sub) * tq
            self.rows = pl.ds(sub * tq, tq)
            self.m_ref, self.acc_ref = m_sets[sub % 2], acc_sets[sub % 2]
            qn = comp_norm(q_ref[0, self.rows, :].astype(F32), qg_ref[...]) * (LOG2E * dk ** -0.5)
            self.q_plain = (jnp.where(first, qn, 0.0).astype(BF16),
                            jnp.where(first, 0.0, qn).astype(BF16))
            self.q_left = (jnp.where(first, qn, q_feat).astype(BF16),
                           jnp.where(first, q_feat, qn).astype(BF16))
            self.q_right = (jnp.where(first, qn, -q_feat).astype(BF16),
                            jnp.where(first, -q_feat, qn).astype(BF16))
            self.jd = self.q0 // tk
            self.t_row = (jnp.asarray(self.q0, F32)
                          + lax.broadcasted_iota(jnp.int32, (1, tq), 1).astype(F32))

        def chunk_of(self, step, window):
            if step == 0:
                return self.jd, None, None
            if window is None:
                j = (step - 1) + jnp.where((step - 1) >= self.jd, 1, 0)
                return j, j < self.jd, None
            k = (step + 1) // 2
            left = step % 2 == 1
            j = self.jd - k if left else self.jd + k
            valid = (j >= 0) & (j < nk)
            return jnp.clip(j, 0, nk - 1), left, valid

        def scores(self, step, window, z_ref):
            j, left, _ = self.chunk_of(step, window)
            start = pl.multiple_of(j * tk, tk)
            if left is None:
                key = lax.broadcasted_iota(jnp.int32, (tk, tq), 0) + j * tk
                qry = lax.broadcasted_iota(jnp.int32, (tk, tq), 1) + self.q0
                bias = (-slope2) * jnp.abs((qry - key).astype(F32))
            for c_idx in range(2):
                kc = k_ref_map[c_idx][pl.ds(start, tk), :]
                if left is None:
                    z = lax.dot_general(kc, self.q_plain[c_idx], nt,
                                        preferred_element_type=F32) + bias
                else:
                    if isinstance(left, bool):
                        qv = self.q_left[c_idx] if left else self.q_right[c_idx]
                    else:
                        qv = jnp.where(left, self.q_left[c_idx], self.q_right[c_idx])
                    z = lax.dot_general(kc, qv, nt, preferred_element_type=F32)
                z_ref[c_idx][...] = z

        def softmax_pv(self, step, window, z_ref):
            j, left, valid = self.chunk_of(step, window)
            vt = vt_ref[j]
            for c_idx in range(2):
                z = z_ref[c_idx][...]
                if left is None:
                    m_new = jnp.max(z, axis=0, keepdims=True)
                    p = jnp.exp2(z - m_new)
                    self.acc_ref[c_idx][...] = jnp.dot(vt, p.astype(BF16),
                                                       preferred_element_type=F32)
                else:
                    rterm = (jnp.where(left, -slope2, slope2)
                             * (self.t_row - jnp.asarray(j * tk, F32)))
                    if valid is not None:
                        rterm = jnp.where(valid, rterm, -jnp.inf)
                    m_prev = self.m_ref[c_idx][...]
                    m_new = jnp.maximum(m_prev, jnp.max(z, axis=0, keepdims=True) + rterm)
                    alpha = jnp.exp2(m_prev - m_new)
                    p = jnp.exp2(z - (m_new - rterm))
                    self.acc_ref[c_idx][...] = alpha * self.acc_ref[c_idx][...] + jnp.dot(
                        vt, p.astype(BF16), preferred_element_type=F32)
                self.m_ref[c_idx][...] = m_new

        def finish(self):
            acc = self.acc_ref
            o_t = (acc[0][0:dv, :] / acc[0][dv:dv + 1, :]
                   - lam * (acc[1][0:dv, :] / acc[1][dv:dv + 1, :]))
            ms = jnp.mean(o_t * o_t, axis=0, keepdims=True)
            on_t = (o_t * lax.rsqrt(ms + RMS_EPS) * hg_col).astype(BF16)
            o_ref[0, self.rows, :] = lax.dot_general(
                eye_q, on_t, nt, preferred_element_type=F32).astype(o_ref.dtype)

    def run(window):
        n_steps = nk if window is None else 2 * window + 1
        items = [(sub, step) for sub in range(n_sub) for step in range(n_steps)]
        tiles = {0: Tile(0)}
        tiles[0].scores(0, window, z_slots[0])
        pending_finish = None
        for n, (sub, step) in enumerate(items):
            if n + 2 < len(items) and items[n + 2][1] == 0:
                tiles[items[n + 2][0]] = Tile(items[n + 2][0])
            if n + 1 < len(items):
                sub_n, step_n = items[n + 1]
                if sub_n not in tiles:
                    tiles[sub_n] = Tile(sub_n)
                tiles[sub_n].scores(step_n, window, z_slots[(n + 1) % 2])
            tiles[sub].softmax_pv(step, window, z_slots[n % 2])
            if pending_finish is not None:
                pending_finish.finish()
                pending_finish = None
            if step == n_steps - 1:
                pending_finish = tiles[sub]
        pending_finish.finish()

    reach = reach_ref[h]
    windows = [w for w in REACH_WINDOWS if 2 * w + 1 < nk]
    for w in windows:
        pl.when(reach == w)(functools.partial(run, w))
    if windows:
        pl.when(functools.reduce(jnp.logical_and, [reach != w for w in windows]))(
            functools.partial(run, None))
    else:
        run(None)


def _diff_attention(h3, scal, slopes, q_gain, k_gain, head_gain, *, heads, dk, k_blk, v_blk,
                    tq=256, tk=512, n_sub=4):
    B, S, _ = h3.shape
    dv = 2 * dk
    tq, tk = min(tq, S), min(tk, S)
    n_sub = min(n_sub, S // tq)
    assert tk % tq == 0 and S % tk == 0
    assert S % (n_sub * tq) == 0
    qg = jnp.tile(q_gain.astype(F32), 2).reshape(1, dv)
    kg = jnp.tile(k_gain.astype(F32), 2).reshape(1, dv)
    hg = head_gain.astype(F32).reshape(1, dv)
    smem = pl.BlockSpec(memory_space=pltpu.SMEM)
    qk = (1.02 * LOG2E * math.sqrt(dk)) * jnp.max(jnp.abs(q_gain.astype(F32))) * jnp.max(
        jnp.abs(k_gain.astype(F32)))
    reach = jnp.zeros((heads,), jnp.int32)
    for w in sorted(REACH_WINDOWS, reverse=True):
        reach = jnp.where(2.0 * qk - slopes * (LOG2E * (w * tk + 1)) < -UNDERFLOW_LOG2, w, reach)
    rows_ext = dv + 2 * SUBLANES
    tile_scratch = ([pltpu.VMEM((1, tq), F32)] * 4
                    + [pltpu.VMEM((rows_ext, tq), F32)] * 4
                    + [pltpu.VMEM((tk, tq), F32)] * 4)
    return pl.pallas_call(
        functools.partial(_attn_kernel, tq=tq, tk=tk, dk=dk, n_sub=n_sub),
        grid=(B, heads, S // (n_sub * tq)),
        in_specs=[smem, smem, smem,
                  pl.BlockSpec((1, n_sub * tq, dv), lambda b, h, i: (b, i, h)),
                  pl.BlockSpec((1, S, dv), lambda b, h, i: (b, 0, k_blk + h)),
                  pl.BlockSpec((1, S, dv), lambda b, h, i: (b, 0, v_blk + h)),
                  pl.BlockSpec((1, dv), lambda b, h, i: (0, 0)),
                  pl.BlockSpec((1, dv), lambda b, h, i: (0, 0)),
                  pl.BlockSpec((1, dv), lambda b, h, i: (0, 0))],
        out_specs=pl.BlockSpec((1, n_sub * tq, dv), lambda b, h, i: (b, i, h)),
        out_shape=jax.ShapeDtypeStruct((B, S, heads * dv), BF16),
        scratch_shapes=[pltpu.VMEM((S, dv), BF16), pltpu.VMEM((S, dv), BF16),
                        pltpu.VMEM((S // tk, rows_ext, tk), BF16)] + tile_scratch,
        compiler_params=_cparams(("parallel", "parallel", "arbitrary")),
        name="diff_attn",
    )(scal, slopes, reach, h3, h3, h3, qg, kg, hg)


def _complex_powers(a_re, a_im, n):
    p_re = jnp.ones((1,) + a_re.shape, F32)
    p_im = jnp.zeros((1,) + a_re.shape, F32)
    s_re, s_im = a_re, a_im
    while p_re.shape[0] < n + 1:
        q_re = p_re * s_re - p_im * s_im
        q_im = p_re * s_im + p_im * s_re
        p_re = jnp.concatenate([p_re, q_re], axis=0)
        p_im = jnp.concatenate([p_im, q_im], axis=0)
        s_re, s_im = s_re * s_re - s_im * s_im, 2.0 * s_re * s_im
    return p_re[:n + 1], p_im[:n + 1]


def _s5_operators(lam_re, lam_im, log_dt, b_re, b_im, c_re, c_im, d_skip, L):
    hp = lax.Precision.HIGHEST
    G, N, P = b_re.shape[1], b_re.shape[2], b_re.shape[3]
    A = LANES // P
    O = G // A
    AN = A * N
    dt = jnp.exp(log_dt.astype(F32))[:, :, None]
    lr, li = lam_re.astype(F32), lam_im.astype(F32)
    mag = jnp.exp(lr * dt)
    a_re, a_im = mag * jnp.cos(li * dt), mag * jnp.sin(li * dt)
    nr = a_re - 1.0
    den = lr * lr + li * li
    f_re = (nr * lr + a_im * li) / den
    f_im = (a_im * lr - nr * li) / den
    br, bi = b_re.astype(F32), b_im.astype(F32)
    bb_re = f_re[..., None] * br - f_im[..., None] * bi
    bb_im = f_re[..., None] * bi + f_im[..., None] * br
    pw_re, pw_im = _complex_powers(a_re, a_im, L)
    cr, ci = c_re.astype(F32), c_im.astype(F32)
    cv_re = cr[None] * pw_re[:, :, :, None, :] - ci[None] * pw_im[:, :, :, None, :]
    cv_im = cr[None] * pw_im[:, :, :, None, :] + ci[None] * pw_re[:, :, :, None, :]
    ab_re = pw_re[:L, :, :, :, None] * bb_re[None] - pw_im[:L, :, :, :, None] * bb_im[None]
    ab_im = pw_re[:L, :, :, :, None] * bb_im[None] + pw_im[:L, :, :, :, None] * bb_re[None]
    kern = (jnp.einsum('tdgpn,dgnq->tdgpq', cv_re[:L], bb_re, precision=hp)
            - jnp.einsum('tdgpn,dgnq->tdgpq', cv_im[:L], bb_im, precision=hp))

    same_group = (jnp.arange(LANES)[:, None] // P) == (jnp.arange(LANES)[None, :] // P)

    dsk = d_skip.astype(F32).reshape(G, P)
    k0 = kern[0, 0] + kern[0, 1] + dsk[:, :, None] * jnp.eye(P, dtype=F32)[None]
    kfull = jnp.concatenate([kern[1:, 1][::-1], k0[None], kern[1:, 0]], axis=0)
    kq = kfull.reshape(2 * L - 1, O, A, P, P).transpose(0, 1, 2, 4, 3)
    kq = kq.reshape(2 * L - 1, O, LANES, P)
    tiles = jnp.where(same_group, jnp.tile(kq, (1, 1, 1, A)), 0.0).astype(BF16)
    idx = (jnp.arange(L)[None, :] - jnp.arange(L)[:, None]) + (L - 1)
    m8 = tiles[idx].transpose(2, 0, 3, 1, 4).reshape(O, L * LANES, L * LANES)

    state_group = (jnp.arange(LANES)[:, None] // P) == (jnp.arange(AN)[None, :] // N)

    def st_tiles(ab):
        x = ab.reshape(L, O, A, N, P).transpose(0, 1, 2, 4, 3).reshape(L, O, LANES, N)
        x = jnp.where(state_group, jnp.tile(x, (1, 1, 1, A)), 0.0)
        return x.transpose(1, 0, 2, 3).reshape(O, L * LANES, AN)

    def out_tiles(cv):
        x = cv.reshape(L, O, A, P, N).transpose(0, 1, 2, 4, 3).reshape(L, O, AN, P)
        x = jnp.where(state_group.T, jnp.tile(x, (1, 1, 1, A)), 0.0)
        return x.transpose(1, 2, 0, 3).reshape(O, AN, L * LANES)

    ops = []
    for d in range(2):
        order = slice(None, None, -1) if d == 0 else slice(None)
        taus = slice(1, L + 1) if d == 0 else slice(L, 0, -1)
        wst = jnp.concatenate([st_tiles(ab_re[order, d]), st_tiles(ab_im[order, d])],
                              axis=2).astype(BF16)
        wout = jnp.concatenate([out_tiles(cv_re[taus, d]), -out_tiles(cv_im[taus, d])],
                               axis=1).astype(BF16)
        apow = jnp.stack([pw_re[L, d].reshape(O, AN), pw_im[L, d].reshape(O, AN)]
                         + [jnp.zeros((O, AN), F32)] * (SUBLANES - 2), axis=1)
        ops.append((wst, wout, apow))
    return m8, ops[0], ops[1]


def _gelu_tanh(y):
    c = math.sqrt(2.0 / math.pi)
    return 0.5 * y * (1.0 + jnp.tanh(c * (y + 0.044715 * (y * y * y))))


def _s5_pass_kernel(*refs, rows, n_chunks, reverse, with_toeplitz):
    if with_toeplitz:
        x_ref, m_ref, wst_ref, wout_ref, ap_ref, y_ref, h_re_sc, h_im_sc, s_sc, p_sc = refs
    else:
        x_ref, wst_ref, wout_ref, ap_ref, part_ref, y_ref, h_re_sc, h_im_sc, s_sc, p_sc = refs
    n = h_re_sc.shape[1]

    @pl.when(pl.program_id(1) == 0)
    def _():
        h_re_sc[...] = jnp.zeros(h_re_sc.shape, F32)
        h_im_sc[...] = jnp.zeros(h_im_sc.shape, F32)

    x = x_ref[0]
    s_sc[...] = jnp.dot(x, wst_ref[0], preferred_element_type=F32)
    ar = ap_ref[0, 0:1, :]
    ai = ap_ref[0, 1:2, :]

    def step(k, carry):
        h_re, h_im = carry
        c = (n_chunks - 1 - k) if reverse else k
        r = pl.ds(pl.multiple_of(c * rows, rows), rows)
        p_sc[r, 0:n] = h_re
        p_sc[r, n:2 * n] = h_im
        return (ar * h_re - ai * h_im + s_sc[r, 0:n], ar * h_im + ai * h_re + s_sc[r, n:2 * n])

    h_re, h_im = lax.fori_loop(0, n_chunks, step, (h_re_sc[...], h_im_sc[...]))
    h_re_sc[...] = h_re
    h_im_sc[...] = h_im

    y = jnp.dot(p_sc[...].astype(BF16), wout_ref[0], preferred_element_type=F32)
    if with_toeplitz:
        y_ref[0] = y + jnp.dot(x, m_ref[0], preferred_element_type=F32)
    else:
        y_ref[0] = _gelu_tanh(y + part_ref[0]).astype(y_ref.dtype)


def _s5_chunked(x, m8, fwd_ops, bwd_ops, blk0, *, rows, tr=512):
    O, R, W = x.shape
    n2 = fwd_ops[0].shape[2]
    tr = min(tr, R)
    nt = R // tr
    scratch = [pltpu.VMEM((rows, n2 // 2), F32), pltpu.VMEM((rows, n2 // 2), F32),
               pltpu.VMEM((tr, n2), F32), pltpu.VMEM((tr, n2), F32)]

    def row_block(order):
        return pl.BlockSpec((1, tr, W), lambda o, r: (o, order(r), 0))

    def per_block(shape):
        return pl.BlockSpec((1,) + shape, lambda o, r: (blk0 + o, 0, 0))

    wst, wout, apow = fwd_ops
    part = pl.pallas_call(
        functools.partial(_s5_pass_kernel, rows=rows, n_chunks=tr // rows, reverse=False,
                          with_toeplitz=True),
        grid=(O, nt),
        in_specs=[row_block(lambda r: r), per_block((W, W)), per_block((W, n2)),
                  per_block((n2, W)), per_block((SUBLANES, n2 // 2))],
        out_specs=row_block(lambda r: r),
        out_shape=jax.ShapeDtypeStruct((O, R, W), F32),
        scratch_shapes=scratch,
        compiler_params=_cparams(("parallel", "arbitrary")),
        name="s5_forward",
    )(x, m8, wst, wout, apow)
    wst, wout, apow = bwd_ops
    return pl.pallas_call(
        functools.partial(_s5_pass_kernel, rows=rows, n_chunks=tr // rows, reverse=True,
                          with_toeplitz=False),
        grid=(O, nt),
        in_specs=[row_block(lambda r: nt - 1 - r), per_block((W, n2)), per_block((n2, W)),
                  per_block((SUBLANES, n2 // 2)), row_block(lambda r: nt - 1 - r)],
        out_specs=row_block(lambda r: nt - 1 - r),
        out_shape=jax.ShapeDtypeStruct((O, R, W), BF16),
        scratch_shapes=scratch,
        compiler_params=_cparams(("parallel", "arbitrary")),
        name="s5_backward",
    )(x, wst, wout, apow, part)


def _glu_merge_kernel(y_ref, o_ref, ga_ref, gb_ref, wg_ref, wa_ref, wb_ref, out_ref):
    y = y_ref[...]
    z = jnp.dot(y, wg_ref[...], preferred_element_type=F32)
    s = (y.astype(F32) * _sigmoid(z)).astype(BF16)
    a_out = jnp.dot(o_ref[...], wa_ref[...], preferred_element_type=F32)
    b_out = jnp.dot(s, wb_ref[...], preferred_element_type=F32)
    merged = (_sigmoid(ga_ref[...].astype(F32)) * a_out + _sigmoid(gb_ref[...].astype(F32)) * b_out)
    out_ref[...] = merged.astype(out_ref.dtype)


def _glu_merge(y, o, h, wg, wa, wb, *, ga_blk, gb_blk, tm=256):
    T, SW = y.shape
    AW = o.shape[1]
    D = wa.shape[1]
    tm = min(tm, T)
    return pl.pallas_call(
        _glu_merge_kernel,
        grid=(T // tm,),
        in_specs=[pl.BlockSpec((tm, SW), lambda i: (i, 0)),
                  pl.BlockSpec((tm, AW), lambda i: (i, 0)),
                  pl.BlockSpec((tm, D), lambda i: (i, ga_blk)),
                  pl.BlockSpec((tm, D), lambda i: (i, gb_blk)),
                  pl.BlockSpec((SW, SW), lambda i: (0, 0)),
                  pl.BlockSpec((AW, D), lambda i: (0, 0)),
                  pl.BlockSpec((SW, D), lambda i: (0, 0))],
        out_specs=pl.BlockSpec((tm, D), lambda i: (i, 0)),
        out_shape=jax.ShapeDtypeStruct((T, D), BF16),
        compiler_params=_cparams(("parallel",)),
        name="glu_merge",
    )(y, o, h, h, wg, wa, wb)


def _outproj_kernel(m_ref, x_ref, w_ref, o_ref):
    o_ref[...] = x_ref[...] + jnp.dot(m_ref[...], w_ref[...], preferred_element_type=F32)


def _outproj(merged, x2d, w, *, tm=512):
    T, D = x2d.shape
    tm = min(tm, T)
    return pl.pallas_call(
        _outproj_kernel,
        grid=(T // tm,),
        in_specs=[pl.BlockSpec((tm, D), lambda i: (i, 0)),
                  pl.BlockSpec((tm, D), lambda i: (i, 0)),
                  pl.BlockSpec((D, D), lambda i: (0, 0))],
        out_specs=pl.BlockSpec((tm, D), lambda i: (i, 0)),
        out_shape=jax.ShapeDtypeStruct((T, D), F32),
        compiler_params=_cparams(("parallel",)),
        name="outproj",
    )(merged, x2d, w)


def _router_kernel(x_ref, g_ref, whi_ref, wlo_ref, b_ref, xe_ref, meta_ref, cnt_ref,
                   tri_sc, eye_sc, carry_sc, *, n_groups, per_group, e_row0):
    t = pl.program_id(0)
    tm, D = x_ref.shape
    n_rows = whi_ref.shape[0]

    @pl.when(t == 0)
    def _():
        r = lax.broadcasted_iota(jnp.int32, (tm, tm), 0)
        c = lax.broadcasted_iota(jnp.int32, (tm, tm), 1)
        tri_sc[...] = (r < c).astype(BF16)
        eye_sc[...] = (r == c).astype(BF16)
        carry_sc[...] = jnp.zeros(carry_sc.shape, F32)

    x = x_ref[...]
    ms = jnp.mean(x * x, axis=-1, keepdims=True)
    xn = x * lax.rsqrt(ms + RMS_EPS) * g_ref[...]
    xe_ref[:, 0:D] = xn

    nt = (((1,), (1,)), ((), ()))
    x_hi, x_lo = _split_bf16(xn)
    whi, wlo = whi_ref[...], wlo_ref[...]
    logits = (lax.dot_general(whi, x_hi, nt, preferred_element_type=F32)
              + lax.dot_general(whi, x_lo, nt, preferred_element_type=F32)
              + lax.dot_general(wlo, x_hi, nt, preferred_element_type=F32)) + b_ref[...]

    g = [logits[k:k + 1, :] for k in range(n_groups)]
    g_max = functools.reduce(jnp.maximum, g)
    g_idx = jnp.full(g_max.shape, n_groups - 1, jnp.int32)
    for k in range(n_groups - 2, -1, -1):
        g_idx = jnp.where(g[k] == g_max, k, g_idx)
    g_w = 1.0 / functools.reduce(lambda a, b: a + b, [jnp.exp(v - g_max) for v in g])

    e_in = []
    for j in range(per_group):
        v = logits[e_row0 + j:e_row0 + j + 1, :]
        for k in range(1, n_groups):
            row = e_row0 + k * per_group + j
            v = jnp.where(g_idx == k, logits[row:row + 1, :], v)
        e_in.append(v)

    def first_argmax(vals):
        vmax = functools.reduce(jnp.maximum, vals)
        idx = jnp.full(vmax.shape, len(vals) - 1, jnp.int32)
        for k in range(len(vals) - 2, -1, -1):
            idx = jnp.where(vals[k] == vmax, k, idx)
        return vmax, idx

    v1, i1 = first_argmax(e_in)
    rest = [jnp.where(i1 == j, -jnp.inf, e_in[j]) for j in range(per_group)]
    v2, i2 = first_argmax(rest)
    e2 = jnp.exp(v2 - v1)
    w1 = g_w / (1.0 + e2)
    w2 = g_w * e2 / (1.0 + e2)
    lo_first = i1 < i2
    a = jnp.where(lo_first, i1, i2)
    b = jnp.where(lo_first, i2, i1)
    w_a = jnp.where(lo_first, w1, w2)
    w_b = jnp.where(lo_first, w2, w1)
    pair = jnp.where(a == 0, b - 1, jnp.where(a == 1, b + 1, 5))
    bucket = g_idx * N_PAIRS + pair

    rows = lax.broadcasted_iota(jnp.int32, (n_rows, tm), 0)
    onehot = (rows == bucket).astype(F32)
    prefix = jnp.dot(onehot.astype(BF16), tri_sc[...], preferred_element_type=F32)
    rank = jnp.sum(onehot * (prefix + carry_sc[...]), axis=0, keepdims=True)
    carry_sc[...] = carry_sc[...] + jnp.sum(onehot, axis=1, keepdims=True)
    cnt_ref[...] = jnp.broadcast_to(carry_sc[...], cnt_ref.shape)

    meta_ref[0, 0:1, :] = bucket.astype(F32)
    meta_ref[0, 1:2, :] = rank
    meta_ref[0, 2:3, :] = w_a
    meta_ref[0, 3:4, :] = w_b
    meta_ref[0, 4:8, :] = jnp.zeros((4, tm), F32)

    eye = eye_sc[...]
    for blk, w in enumerate((w_a, w_b)):
        wrep = jnp.broadcast_to(w, (LANES, tm))
        hi, lo = _split_bf16(wrep)
        col = (lax.dot_general(eye, hi, nt, preferred_element_type=F32)
               + lax.dot_general(eye, lo, nt, preferred_element_type=F32))
        xe_ref[:, D + blk * LANES:D + (blk + 1) * LANES] = col


def _router(x2d, gain, w_rg, b_rg, w_re, b_re, *, tm=512):
    T, D = x2d.shape
    n_groups = w_rg.shape[1]
    n_exp = w_re.shape[1]
    per_group = n_exp // n_groups
    e_row0 = SUBLANES
    n_rows = e_row0 + n_exp
    n_rows = ((n_rows + SUBLANES - 1) // SUBLANES) * SUBLANES
    assert n_groups * N_PAIRS <= n_rows and per_group == 4
    wt = jnp.zeros((n_rows, D), F32)
    wt = wt.at[0:n_groups].set(w_rg.astype(F32).T).at[e_row0:e_row0 + n_exp].set(w_re.astype(F32).T)
    bias = jnp.zeros((n_rows, 1), F32)
    bias = bias.at[0:n_groups, 0].set(b_rg.astype(F32)).at[e_row0:e_row0 + n_exp, 0].set(b_re.astype(F32))
    w_hi, w_lo = _split_bf16(wt)
    tm = min(tm, T)
    De = D + 2 * LANES
    return pl.pallas_call(
        functools.partial(_router_kernel, n_groups=n_groups, per_group=per_group, e_row0=e_row0),
        grid=(T // tm,),
        in_specs=[pl.BlockSpec((tm, D), lambda t: (t, 0)),
                  pl.BlockSpec((1, D), lambda t: (0, 0)),
                  pl.BlockSpec((n_rows, D), lambda t: (0, 0)),
                  pl.BlockSpec((n_rows, D), lambda t: (0, 0)),
                  pl.BlockSpec((n_rows, 1), lambda t: (0, 0))],
        out_specs=[pl.BlockSpec((tm, De), lambda t: (t, 0)),
                   pl.BlockSpec((1, SUBLANES, tm), lambda t: (t, 0, 0)),
                   pl.BlockSpec((n_rows, LANES), lambda t: (0, 0))],
        out_shape=[jax.ShapeDtypeStruct((T, De), F32),
                   jax.ShapeDtypeStruct((T // tm, SUBLANES, tm), F32),
                   jax.ShapeDtypeStruct((n_rows, LANES), F32)],
        scratch_shapes=[pltpu.VMEM((tm, tm), BF16), pltpu.VMEM((tm, tm), BF16),
                        pltpu.VMEM((n_rows, 1), F32)],
        compiler_params=_cparams(("arbitrary",)),
        name="router",
    )(x2d, gain.reshape(1, D).astype(F32), w_hi, w_lo, bias)


def _dispatch_kernel(pos_ref, x_ref, xs_in_ref, xs_ref, sem, *, tm):
    del xs_in_ref
    base = pl.program_id(0) * tm

    def issue(r2, carry):
        for lane in range(2):
            r = 2 * r2 + lane
            pltpu.make_async_copy(x_ref.at[pl.ds(r, 1)], xs_ref.at[pl.ds(pos_ref[base + r], 1)],
                                  sem).start(priority=lane)
        return carry

    lax.fori_loop(0, tm // 2, issue, 0, unroll=4)

    def drain(r, carry):
        pltpu.make_async_copy(x_ref.at[pl.ds(0, 1)], xs_ref.at[pl.ds(0, 1)], sem).wait()
        return carry

    lax.fori_loop(0, tm, drain, 0, unroll=8)


def _dispatch(pos, xe, n_rows, *, tm=512):
    T, De = xe.shape
    tm = min(tm, T)
    xs0 = jnp.zeros((n_rows, De), F32)
    return pl.pallas_call(
        functools.partial(_dispatch_kernel, tm=tm),
        grid_spec=pltpu.PrefetchScalarGridSpec(
            num_scalar_prefetch=1,
            grid=(T // tm,),
            in_specs=[pl.BlockSpec((tm, De), lambda t, pos: (t, 0)),
                      pl.BlockSpec(memory_space=pl.ANY)],
            out_specs=pl.BlockSpec(memory_space=pl.ANY),
            scratch_shapes=[pltpu.SemaphoreType.DMA(())]),
        out_shape=jax.ShapeDtypeStruct((n_rows, De), F32),
        input_output_aliases={2: 0},
        compiler_params=_cparams(("arbitrary",)),
        name="moe_dispatch",
    )(pos, xe, xs0)


def _collect_kernel(pos_ref, ys_ref, x_ref, o_ref, buf, sem, *, tm):
    base = pl.program_id(0) * tm

    def issue(r2, carry):
        for lane in range(2):
            r = 2 * r2 + lane
            pltpu.make_async_copy(ys_ref.at[pl.ds(pos_ref[base + r], 1)], buf.at[pl.ds(r, 1)],
                                  sem).start(priority=lane)
        return carry

    lax.fori_loop(0, tm // 2, issue, 0, unroll=4)

    def drain(r, carry):
        pltpu.make_async_copy(ys_ref.at[pl.ds(0, 1)], buf.at[pl.ds(0, 1)], sem).wait()
        return carry

    lax.fori_loop(0, tm, drain, 0, unroll=8)
    o_ref[...] = x_ref[...] + buf[...]


def _collect(pos, ys, x2d, *, tm=512):
    T, D = x2d.shape
    tm = min(tm, T)
    return pl.pallas_call(
        functools.partial(_collect_kernel, tm=tm),
        grid_spec=pltpu.PrefetchScalarGridSpec(
            num_scalar_prefetch=1,
            grid=(T // tm,),
            in_specs=[pl.BlockSpec(memory_space=pl.ANY),
                      pl.BlockSpec((tm, D), lambda t, pos: (t, 0))],
            out_specs=pl.BlockSpec((tm, D), lambda t, pos: (t, 0)),
            scratch_shapes=[pltpu.VMEM((tm, D), F32), pltpu.SemaphoreType.DMA(())]),
        out_shape=jax.ShapeDtypeStruct((T, D), F32),
        compiler_params=_cparams(("arbitrary",)),
        name="moe_collect",
    )(pos, ys, x2d)


def _ffn_kernel(te_ref, tv_ref, x_ref, w1_ref, w3_ref, w2_ref, o_ref, xb_sc, *, D):
    i, j, f = pl.program_id(0), pl.program_id(1), pl.program_id(2)
    first = jnp.logical_and(j == 0, f == 0)

    @pl.when(first)
    def _():
        xb_sc[...] = x_ref[:, 0:D].astype(BF16)
        o_ref[...] = jnp.zeros(o_ref.shape, F32)

    @pl.when(tv_ref[i] != 0)
    def _():
        xb = xb_sc[...]
        h1 = jnp.dot(xb, w1_ref[0], preferred_element_type=F32)
        h3 = jnp.dot(xb, w3_ref[0], preferred_element_type=F32)
        hid = (h1 * _sigmoid(h1) * h3).astype(BF16)
        y = jnp.dot(hid, w2_ref[0], preferred_element_type=F32)
        w = jnp.where(j == 0, x_ref[:, D:D + 1], x_ref[:, D + LANES:D + LANES + 1])
        o_ref[...] += w * y


def _expert_ffn(tile_expert, tile_valid, xs, w1, w3, w2, *, tm, tf=512):
    R, De = xs.shape
    E, D, FF = w1.shape
    tf = min(tf, FF)
    n_tiles = R // tm
    return pl.pallas_call(
        functools.partial(_ffn_kernel, D=D),
        grid_spec=pltpu.PrefetchScalarGridSpec(
            num_scalar_prefetch=2,
            grid=(n_tiles, 2, FF // tf),
            in_specs=[pl.BlockSpec((tm, De), lambda i, j, f, te, tv: (i, 0)),
                      pl.BlockSpec((1, D, tf), lambda i, j, f, te, tv: (te[2 * i + j], 0, f)),
                      pl.BlockSpec((1, D, tf), lambda i, j, f, te, tv: (te[2 * i + j], 0, f)),
                      pl.BlockSpec((1, tf, D), lambda i, j, f, te, tv: (te[2 * i + j], f, 0))],
            out_specs=pl.BlockSpec((tm, D), lambda i, j, f, te, tv: (i, 0)),
            scratch_shapes=[pltpu.VMEM((tm, D), BF16)]),
        out_shape=jax.ShapeDtypeStruct((R, D), F32),
        compiler_params=_cparams(("parallel", "arbitrary", "arbitrary")),
        name="expert_ffn",
    )(tile_expert, tile_valid, xs, w1, w3, w2)


def _moe_plan(meta, counts, n_groups, per_group, T, tm_e):
    n_buckets = n_groups * N_PAIRS
    bucket = meta[:, 0, :].reshape(T).astype(jnp.int32)
    rank = meta[:, 1, :].reshape(T).astype(jnp.int32)
    cnt = counts[:n_buckets, 0].astype(jnp.int32)
    padded = ((cnt + tm_e - 1) // tm_e) * tm_e
    ends = jnp.cumsum(padded)
    offs = ends - padded
    pos = offs[bucket] + rank
    n_tiles = T // tm_e + n_buckets
    starts = jnp.arange(n_tiles, dtype=jnp.int32) * tm_e
    tile_valid = (starts < ends[-1]).astype(jnp.int32)
    last_start = jnp.maximum(ends[-1] - tm_e, 0)
    tb = jnp.sum((ends[None, :] <= jnp.minimum(starts, last_start)[:, None]).astype(jnp.int32), axis=1)
    tb = jnp.minimum(tb, n_buckets - 1)
    grp, pair = tb // N_PAIRS, tb % N_PAIRS
    ea = grp * per_group + jnp.asarray(_PAIR_A, jnp.int32)[pair]
    eb = grp * per_group + jnp.asarray(_PAIR_B, jnp.int32)[pair]
    tile_expert = jnp.stack([ea, eb], axis=1).reshape(-1)
    return pos, tile_expert, tile_valid, n_tiles


def _hier_moe_residual(x2d, gain, w_rg, b_rg, w_re, b_re, w1, w3, w2, expert0, *, tm_e=512):
    T, D = x2d.shape
    n_groups = w_rg.shape[1]
    per_group = w_re.shape[1] // n_groups
    tm_e = min(tm_e, T)
    xe, meta, counts = _router(x2d, gain, w_rg, b_rg, w_re, b_re)
    pos, tile_expert, tile_valid, n_tiles = _moe_plan(meta, counts, n_groups, per_group, T, tm_e)
    xs = _dispatch(pos, xe, n_tiles * tm_e)
    ys = _expert_ffn(tile_expert + expert0, tile_valid, xs, w1, w3, w2, tm=tm_e)
    return _collect(pos, ys, x2d)


def _hybrid_mixer_residual(x2d, B, S, layer_idx, slopes, norm1, w_in, q_gain, k_gain, lam_q1, lam_k1,
                           lam_q2, lam_k2, head_gain, w_attn_up, s5_ops, w_glu, w_ssm_up, w_out):
    T, D = x2d.shape
    dk = q_gain.shape[0]
    dv = head_gain.shape[0]
    AW, SW = w_attn_up.shape[0], w_glu.shape[0]
    heads = AW // dv
    assert dv == 2 * dk == LANES
    u_off = 3 * AW
    ga_off = u_off + SW
    gb_off = ga_off + D
    assert ga_off % D == 0

    h = _norm_inproj(x2d, norm1, w_in.astype(BF16))

    lam_init = 0.8 - 0.6 * math.exp(-0.3 * layer_idx)
    lam = (jnp.exp(jnp.sum(lam_q1.astype(F32) * lam_k1.astype(F32)))
           - jnp.exp(jnp.sum(lam_q2.astype(F32) * lam_k2.astype(F32))) + lam_init)
    scal = jnp.stack([lam, jnp.asarray(1.0 - lam_init, F32)]).astype(F32)
    o = _diff_attention(h.reshape(B, S, -1), scal, slopes, q_gain, k_gain, head_gain,
                        heads=heads, dk=dk, k_blk=AW // dv, v_blk=2 * AW // dv)
    o = o.reshape(T, AW)

    L = min(S5_CHUNK, S)
    nc = S // L
    n_blk = SW // LANES
    assert B % SUBLANES == 0 and SW % LANES == 0
    m8, fwd_ops, bwd_ops = s5_ops
    u = h[:, u_off:u_off + SW].reshape(B, nc, L, n_blk, LANES)
    ut = u.transpose(3, 1, 0, 2, 4).reshape(n_blk, nc * B, L * LANES)
    yt = _s5_chunked(ut, m8, fwd_ops, bwd_ops, layer_idx * n_blk, rows=B)
    y = yt.reshape(n_blk, nc, B, L, LANES).transpose(2, 1, 3, 0, 4).reshape(T, SW)

    merged = _glu_merge(y, o, h, w_glu.astype(BF16), w_attn_up.astype(BF16), w_ssm_up.astype(BF16),
                        ga_blk=ga_off // D, gb_blk=gb_off // D)
    return _outproj(merged, x2d, w_out.astype(BF16))


def kernel(x, norm1, w_in, q_gain, k_gain, lam_q1, lam_k1, lam_q2, lam_k2, head_gain, w_attn_up, ssm_lam_re, ssm_lam_im, ssm_log_dt, ssm_b_re, ssm_b_im, ssm_c_re, ssm_c_im, ssm_d, w_glu, w_ssm_up, w_out, norm2, w_router_group, b_router_group, w_router_expert, b_router_expert, w1, w3, w2):
    B, S, D = x.shape
    depth = norm1.shape[0]
    heads = w_attn_up.shape[1] // head_gain.shape[1]
    slopes = jnp.exp2(-8.0 * jnp.arange(1, heads + 1, dtype=F32) / heads)
    x2d = x.reshape(B * S, D).astype(F32)
    n_exp = w1.shape[1]
    w1b, w3b, w2b = (w.astype(BF16).reshape((depth * n_exp,) + w.shape[2:]) for w in (w1, w3, w2))
    assert LANES % ssm_b_re.shape[4] == 0
    s5_all = jax.vmap(functools.partial(_s5_operators, L=min(S5_CHUNK, S)))(
        ssm_lam_re, ssm_lam_im, ssm_log_dt, ssm_b_re, ssm_b_im, ssm_c_re, ssm_c_im, ssm_d)
    s5_ops = jax.tree.map(lambda a: a.reshape((a.shape[0] * a.shape[1],) + a.shape[2:]), s5_all)
    for l in range(depth):
        x2d = _hybrid_mixer_residual(
            x2d, B, S, l, slopes, norm1[l], w_in[l], q_gain[l], k_gain[l], lam_q1[l], lam_k1[l],
            lam_q2[l], lam_k2[l], head_gain[l], w_attn_up[l], s5_ops, w_glu[l], w_ssm_up[l], w_out[l])
        x2d = _hier_moe_residual(
            x2d, norm2[l], w_router_group[l], b_router_group[l], w_router_expert[l],
            b_router_expert[l], w1b, w3b, w2b, l * n_exp)
    return x2d.reshape(B, S, D).astype(x.dtype)
```

```python
import functools
import math

import jax
import jax.numpy as jnp
from jax import lax
from jax.experimental import pallas as pl
from jax.experimental.pallas import tpu as pltpu

F32 = jnp.float32
BF16 = jnp.bfloat16
RMS_EPS = 1e-6
LOG2E = math.log2(math.e)
LANES = 128
SUBLANES = 8
VMEM_LIMIT = 56 * 1024 * 1024
S5_CHUNK = 8
N_PAIRS = 6
_PAIR_A = (0, 0, 0, 1, 1, 2)
_PAIR_B = (1, 2, 3, 2, 3, 3)


def _cparams(sem):
    return pltpu.CompilerParams(dimension_semantics=sem, vmem_limit_bytes=VMEM_LIMIT)


def _sigmoid(x):
    return 1.0 / (1.0 + jnp.exp(-x))


def _pick_tile(n, target, unit=LANES):
    if n <= target:
        return n
    t = (target // unit) * unit
    while n % t:
        t -= unit
    return t


def _split_bf16(x):
    hi = x.astype(BF16)
    lo = (x - hi.astype(F32)).astype(BF16)
    return hi, lo


def _norm_inproj_kernel(x_ref, g_ref, w_ref, o_ref, xn_ref):
    @pl.when(pl.program_id(1) == 0)
    def _():
        x = x_ref[...]
        ms = jnp.mean(x * x, axis=-1, keepdims=True)
        xn_ref[...] = (x * lax.rsqrt(ms + RMS_EPS) * g_ref[...]).astype(BF16)

    o_ref[...] = jnp.dot(xn_ref[...], w_ref[...], preferred_element_type=F32).astype(o_ref.dtype)


def _norm_inproj(x2d, gain, w, *, tm=1024, tn=1024):
    T, D = x2d.shape
    N = w.shape[1]
    tm, tn = _pick_tile(T, tm, SUBLANES), _pick_tile(N, tn)
    return pl.pallas_call(
        _norm_inproj_kernel,
        grid=(T // tm, N // tn),
        in_specs=[pl.BlockSpec((tm, D), lambda i, j: (i, 0)),
                  pl.BlockSpec((1, D), lambda i, j: (0, 0)),
                  pl.BlockSpec((D, tn), lambda i, j: (0, j))],
        out_specs=pl.BlockSpec((tm, tn), lambda i, j: (i, j)),
        out_shape=jax.ShapeDtypeStruct((T, N), BF16),
        scratch_shapes=[pltpu.VMEM((tm, D), BF16)],
        compiler_params=_cparams(("parallel", "arbitrary")),
        name="norm_inproj",
    )(x2d, gain.reshape(1, D).astype(F32), w)


N_BIAS_LANES = 6
UNDERFLOW_LOG2 = 152.0
REACH_WINDOWS = (1, 2)


def _attn_kernel(scal_ref, slope_ref, reach_ref, q_ref, k_ref, v_ref, qg_ref, kg_ref, hg_ref, o_ref,
                 ka_ref, kb_ref, vt_ref, *tile_scratch, tq, tk, dk, n_sub):
    k_ref_map = (ka_ref, kb_ref)
    h = pl.program_id(1)
    i = pl.program_id(2)
    S = k_ref.shape[1]
    dv = 2 * dk
    nk = S // tk
    nt = (((1,), (1,)), ((), ()))
    lane = lax.broadcasted_iota(jnp.int32, (1, dv), 1)
    first = lane < dk
    feat_lane = jnp.where(first, lane, lane - dk)
    r = lax.broadcasted_iota(jnp.int32, (dv, dv), 0)
    c = lax.broadcasted_iota(jnp.int32, (dv, dv), 1)
    ones_bd = ((r < dk) == (c < dk)).astype(BF16)
    eye_dv = (r == c).astype(BF16)

    def comp_norm(x, gain):
        sq_hi, sq_lo = _split_bf16(x * x)
        ss = (jnp.dot(sq_hi, ones_bd, preferred_element_type=F32)
              + jnp.dot(sq_lo, ones_bd, preferred_element_type=F32))
        return x * lax.rsqrt(ss * (1.0 / dk) + RMS_EPS) * gain

    @pl.when(i == 0)
    def _():
        kn = comp_norm(k_ref[0].astype(F32), kg_ref[...])
        ds = lax.broadcasted_iota(jnp.int32, (S, dv), 0) & (tk - 1)
        ds_lo = ds & 255
        fl = jnp.broadcast_to(feat_lane, (S, dv))
        feat = jnp.where(fl < 3, ds_lo, jnp.where(fl < N_BIAS_LANES, ds - ds_lo, 0)).astype(F32)
        ka_ref[...] = jnp.where(first, kn, feat).astype(BF16)
        kb_ref[...] = jnp.where(first, feat, kn).astype(BF16)
        for jc in range(nk):
            vt_ref[jc, 0:dv, :] = lax.dot_general(
                eye_dv, v_ref[0, jc * tk:(jc + 1) * tk, :], nt,
                preferred_element_type=F32).astype(BF16)
            vt_ref[jc, dv:dv + 2 * SUBLANES, :] = jnp.ones((2 * SUBLANES, tk), BF16)

    slope2 = slope_ref[h] * LOG2E
    s_full = jnp.full((1, dv), slope2, F32)
    s_hi = s_full.astype(BF16).astype(F32)
    s_mid = (s_full - s_hi).astype(BF16).astype(F32)
    s_lo = (s_full - s_hi - s_mid).astype(BF16).astype(F32)
    q_feat = jnp.where((feat_lane == 0) | (feat_lane == 3), s_hi,
                       jnp.where((feat_lane == 1) | (feat_lane == 4), s_mid,
                                 jnp.where((feat_lane == 2) | (feat_lane == 5), s_lo, 0.0)))

    lam = scal_ref[0]
    out_scale = scal_ref[1]
    rq = lax.broadcasted_iota(jnp.int32, (tq, tq), 0)
    cq = lax.broadcasted_iota(jnp.int32, (tq, tq), 1)
    eye_q = (rq == cq).astype(BF16)
    hg_col = jnp.sum(jnp.where(r == c, jnp.broadcast_to(hg_ref[...], (dv, dv)), 0.0),
                     axis=1, keepdims=True) * out_scale

    m_sets = (tile_scratch[0:2], tile_scratch[2:4])
    acc_sets = (tile_scratch[4:6], tile_scratch[6:8])
    z_slots = (tile_scratch[8:10], tile_scratch[10:12])

    class Tile:
        def __init__(self, sub):
            self.sub = sub
            self.q0 = (i * n_sub + sub) * tq
            self.rows = pl.ds(sub * tq, tq)
            self.m_ref, self.acc_ref = m_sets[sub % 2], acc_sets[sub % 2]
            qn = comp_norm(q_ref[0, self.rows, :].astype(F32), qg_ref[...]) * (LOG2E * dk ** -0.5)
            self.q_plain = (jnp.where(first, qn, 0.0).astype(BF16),
                            jnp.where(first, 0.0, qn).astype(BF16))
            self.q_left = (jnp.where(first, qn, q_feat).astype(BF16),
                           jnp.where(first, q_feat, qn).astype(BF16))
            self.q_right = (jnp.where(first, qn, -q_feat).astype(BF16),
                            jnp.where(first, -q_feat, qn).astype(BF16))
            self.jd = self.q0 // tk
            self.t_row = (jnp.asarray(self.q0, F32)
                          + lax.broadcasted_iota(jnp.int32, (1, tq), 1).astype(F32))

        def chunk_of(self, step, window):
            if step == 0:
                return self.jd, None, None
            if window is None:
                j = (step - 1) + jnp.where((step - 1) >= self.jd, 1, 0)
                return j, j < self.jd, None
            k = (step + 1) // 2
            left = step % 2 == 1
            j = self.jd - k if left else self.jd + k
            valid = (j >= 0) & (j < nk)
            return jnp.clip(j, 0, nk - 1), left, valid

        def scores(self, step, window, z_ref):
            j, left, _ = self.chunk_of(step, window)
            start = pl.multiple_of(j * tk, tk)
            if left is None:
                key = lax.broadcasted_iota(jnp.int32, (tk, tq), 0) + j * tk
                qry = lax.broadcasted_iota(jnp.int32, (tk, tq), 1) + self.q0
                bias = (-slope2) * jnp.abs((qry - key).astype(F32))
            for c_idx in range(2):
                kc = k_ref_map[c_idx][pl.ds(start, tk), :]
                if left is None:
                    z = lax.dot_general(kc, self.q_plain[c_idx], nt,
                                        preferred_element_type=F32) + bias
                else:
                    if isinstance(left, bool):
                        qv = self.q_left[c_idx] if left else self.q_right[c_idx]
                    else:
                        qv = jnp.where(left, self.q_left[c_idx], self.q_right[c_idx])
                    z = lax.dot_general(kc, qv, nt, preferred_element_type=F32)
                z_ref[c_idx][...] = z

        def softmax_pv(self, step, window, z_ref):
            j, left, valid = self.chunk_of(step, window)
            vt = vt_ref[j]
            for c_idx in range(2):
                z = z_ref[c_idx][...]
                if left is None:
                    m_new = jnp.max(z, axis=0, keepdims=True)
                    p = jnp.exp2(z - m_new)
                    self.acc_ref[c_idx][...] = jnp.dot(vt, p.astype(BF16),
                                                       preferred_element_type=F32)
                else:
                    rterm = (jnp.where(left, -slope2, slope2)
                             * (self.t_row - jnp.asarray(j * tk, F32)))
                    if valid is not None:
                        rterm = jnp.where(valid, rterm, -jnp.inf)
                    m_prev = self.m_ref[c_idx][...]
                    m_new = jnp.maximum(m_prev, jnp.max(z, axis=0, keepdims=True) + rterm)
                    alpha = jnp.exp2(m_prev - m_new)
                    p = jnp.exp2(z - (m_new - rterm))
                    self.acc_ref[c_idx][...] = alpha * self.acc_ref[c_idx][...] + jnp.dot(
                        vt, p.astype(BF16), preferred_element_type=F32)
                self.m_ref[c_idx][...] = m_new

        def finish(self):
            acc = self.acc_ref
            o_t = (acc[0][0:dv, :] / acc[0][dv:dv + 1, :]
                   - lam * (acc[1][0:dv, :] / acc[1][dv:dv + 1, :]))
            ms = jnp.mean(o_t * o_t, axis=0, keepdims=True)
            on_t = (o_t * lax.rsqrt(ms + RMS_EPS) * hg_col).astype(BF16)
            o_ref[0, self.rows, :] = lax.dot_general(
                eye_q, on_t, nt, preferred_element_type=F32).astype(o_ref.dtype)

    def run(window):
        n_steps = nk if window is None else 2 * window + 1
        items = [(sub, step) for sub in range(n_sub) for step in range(n_steps)]
        tiles = {0: Tile(0)}
        tiles[0].scores(0, window, z_slots[0])
        pending_finish = None
        for n, (sub, step) in enumerate(items):
            if n + 2 < len(items) and items[n + 2][1] == 0:
                tiles[items[n + 2][0]] = Tile(items[n + 2][0])
            if n + 1 < len(items):
                sub_n, step_n = items[n + 1]
                if sub_n not in tiles:
                    tiles[sub_n] = Tile(sub_n)
                tiles[sub_n].scores(step_n, window, z_slots[(n + 1) % 2])
            tiles[sub].softmax_pv(step, window, z_slots[n % 2])
            if pending_finish is not None:
                pending_finish.finish()
                pending_finish = None
            if step == n_steps - 1:
                pending_finish = tiles[sub]
        pending_finish.finish()

    reach = reach_ref[h]
    windows = [w for w in REACH_WINDOWS if 2 * w + 1 < nk]
    for w in windows:
        pl.when(reach == w)(functools.partial(run, w))
    if windows:
        pl.when(functools.reduce(jnp.logical_and, [reach != w for w in windows]))(
            functools.partial(run, None))
    else:
        run(None)


def _diff_attention(h3, scal, slopes, q_gain, k_gain, head_gain, *, heads, dk, k_blk, v_blk,
                    tq=256, tk=512, n_sub=8):
    B, S, _ = h3.shape
    dv = 2 * dk
    tq, tk = min(tq, S), min(tk, S)
    n_sub = min(n_sub, S // tq)
    assert tk % tq == 0 and S % tk == 0
    assert S % (n_sub * tq) == 0
    qg = jnp.tile(q_gain.astype(F32), 2).reshape(1, dv)
    kg = jnp.tile(k_gain.astype(F32), 2).reshape(1, dv)
    hg = head_gain.astype(F32).reshape(1, dv)
    smem = pl.BlockSpec(memory_space=pltpu.SMEM)
    qk = (1.02 * LOG2E * math.sqrt(dk)) * jnp.max(jnp.abs(q_gain.astype(F32))) * jnp.max(
        jnp.abs(k_gain.astype(F32)))
    reach = jnp.zeros((heads,), jnp.int32)
    for w in sorted(REACH_WINDOWS, reverse=True):
        reach = jnp.where(2.0 * qk - slopes * (LOG2E * (w * tk + 1)) < -UNDERFLOW_LOG2, w, reach)
    rows_ext = dv + 2 * SUBLANES
    tile_scratch = ([pltpu.VMEM((1, tq), F32)] * 4
                    + [pltpu.VMEM((rows_ext, tq), F32)] * 4
                    + [pltpu.VMEM((tk, tq), F32)] * 4)
    return pl.pallas_call(
        functools.partial(_attn_kernel, tq=tq, tk=tk, dk=dk, n_sub=n_sub),
        grid=(B, heads, S // (n_sub * tq)),
        in_specs=[smem, smem, smem,
                  pl.BlockSpec((1, n_sub * tq, dv), lambda b, h, i: (b, i, h)),
                  pl.BlockSpec((1, S, dv), lambda b, h, i: (b, 0, k_blk + h)),
                  pl.BlockSpec((1, S, dv), lambda b, h, i: (b, 0, v_blk + h)),
                  pl.BlockSpec((1, dv), lambda b, h, i: (0, 0)),
                  pl.BlockSpec((1, dv), lambda b, h, i: (0, 0)),
                  pl.BlockSpec((1, dv), lambda b, h, i: (0, 0))],
        out_specs=pl.BlockSpec((1, n_sub * tq, dv), lambda b, h, i: (b, i, h)),
        out_shape=jax.ShapeDtypeStruct((B, S, heads * dv), BF16),
        scratch_shapes=[pltpu.VMEM((S, dv), BF16), pltpu.VMEM((S, dv), BF16),
                        pltpu.VMEM((S // tk, rows_ext, tk), BF16)] + tile_scratch,
        compiler_params=_cparams(("parallel", "parallel", "arbitrary")),
        name="diff_attn",
    )(scal, slopes, reach, h3, h3, h3, qg, kg, hg)


def _complex_powers(a_re, a_im, n):
    p_re = jnp.ones((1,) + a_re.shape, F32)
    p_im = jnp.zeros((1,) + a_re.shape, F32)
    s_re, s_im = a_re, a_im
    while p_re.shape[0] < n + 1:
        q_re = p_re * s_re - p_im * s_im
        q_im = p_re * s_im + p_im * s_re
        p_re = jnp.concatenate([p_re, q_re], axis=0)
        p_im = jnp.concatenate([p_im, q_im], axis=0)
        s_re, s_im = s_re * s_re - s_im * s_im, 2.0 * s_re * s_im
    return p_re[:n + 1], p_im[:n + 1]


def _s5_operators(lam_re, lam_im, log_dt, b_re, b_im, c_re, c_im, d_skip, L):
    hp = lax.Precision.HIGHEST
    G, N, P = b_re.shape[1], b_re.shape[2], b_re.shape[3]
    A = LANES // P
    O = G // A
    AN = A * N
    dt = jnp.exp(log_dt.astype(F32))[:, :, None]
    lr, li = lam_re.astype(F32), lam_im.astype(F32)
    mag = jnp.exp(lr * dt)
    a_re, a_im = mag * jnp.cos(li * dt), mag * jnp.sin(li * dt)
    nr = a_re - 1.0
    den = lr * lr + li * li
    f_re = (nr * lr + a_im * li) / den
    f_im = (a_im * lr - nr * li) / den
    br, bi = b_re.astype(F32), b_im.astype(F32)
    bb_re = f_re[..., None] * br - f_im[..., None] * bi
    bb_im = f_re[..., None] * bi + f_im[..., None] * br
    pw_re, pw_im = _complex_powers(a_re, a_im, L)
    cr, ci = c_re.astype(F32), c_im.astype(F32)
    cv_re = cr[None] * pw_re[:, :, :, None, :] - ci[None] * pw_im[:, :, :, None, :]
    cv_im = cr[None] * pw_im[:, :, :, None, :] + ci[None] * pw_re[:, :, :, None, :]
    ab_re = pw_re[:L, :, :, :, None] * bb_re[None] - pw_im[:L, :, :, :, None] * bb_im[None]
    ab_im = pw_re[:L, :, :, :, None] * bb_im[None] + pw_im[:L, :, :, :, None] * bb_re[None]
    kern = (jnp.einsum('tdgpn,dgnq->tdgpq', cv_re[:L], bb_re, precision=hp)
            - jnp.einsum('tdgpn,dgnq->tdgpq', cv_im[:L], bb_im, precision=hp))

    same_group = (jnp.arange(LANES)[:, None] // P) == (jnp.arange(LANES)[None, :] // P)

    dsk = d_skip.astype(F32).reshape(G, P)
    k0 = kern[0, 0] + kern[0, 1] + dsk[:, :, None] * jnp.eye(P, dtype=F32)[None]
    kfull = jnp.concatenate([kern[1:, 1][::-1], k0[None], kern[1:, 0]], axis=0)
    kq = kfull.reshape(2 * L - 1, O, A, P, P).transpose(0, 1, 2, 4, 3)
    kq = kq.reshape(2 * L - 1, O, LANES, P)
    tiles = jnp.where(same_group, jnp.tile(kq, (1, 1, 1, A)), 0.0).astype(BF16)
    idx = (jnp.arange(L)[None, :] - jnp.arange(L)[:, None]) + (L - 1)
    m8 = tiles[idx].transpose(2, 0, 3, 1, 4).reshape(O, L * LANES, L * LANES)

    state_group = (jnp.arange(LANES)[:, None] // P) == (jnp.arange(AN)[None, :] // N)

    def st_tiles(ab):
        x = ab.reshape(L, O, A, N, P).transpose(0, 1, 2, 4, 3).reshape(L, O, LANES, N)
        x = jnp.where(state_group, jnp.tile(x, (1, 1, 1, A)), 0.0)
        return x.transpose(1, 0, 2, 3).reshape(O, L * LANES, AN)

    def out_tiles(cv):
        x = cv.reshape(L, O, A, P, N).transpose(0, 1, 2, 4, 3).reshape(L, O, AN, P)
        x = jnp.where(state_group.T, jnp.tile(x, (1, 1, 1, A)), 0.0)
        return x.transpose(1, 2, 0, 3).reshape(O, AN, L * LANES)

    ops = []
    for d in range(2):
        order = slice(None, None, -1) if d == 0 else slice(None)
        taus = slice(1, L + 1) if d == 0 else slice(L, 0, -1)
        wst = jnp.concatenate([st_tiles(ab_re[order, d]), st_tiles(ab_im[order, d])],
                              axis=2).astype(BF16)
        wout = jnp.concatenate([out_tiles(cv_re[taus, d]), -out_tiles(cv_im[taus, d])],
                               axis=1).astype(BF16)
        apow = jnp.stack([pw_re[L, d].reshape(O, AN), pw_im[L, d].reshape(O, AN)]
                         + [jnp.zeros((O, AN), F32)] * (SUBLANES - 2), axis=1)
        ops.append((wst, wout, apow))
    return m8, ops[0], ops[1]


def _gelu_tanh(y):
    c = math.sqrt(2.0 / math.pi)
    return 0.5 * y * (1.0 + jnp.tanh(c * (y + 0.044715 * (y * y * y))))


def _s5_pass_kernel(*refs, rows, n_chunks, reverse, with_toeplitz):
    if with_toeplitz:
        x_ref, m_ref, wst_ref, wout_ref, ap_ref, y_ref, h_re_sc, h_im_sc, s_sc, p_sc = refs
    else:
        x_ref, wst_ref, wout_ref, ap_ref, part_ref, y_ref, h_re_sc, h_im_sc, s_sc, p_sc = refs
    n = h_re_sc.shape[1]

    @pl.when(pl.program_id(1) == 0)
    def _():
        h_re_sc[...] = jnp.zeros(h_re_sc.shape, F32)
        h_im_sc[...] = jnp.zeros(h_im_sc.shape, F32)

    x = x_ref[0]
    s_sc[...] = jnp.dot(x, wst_ref[0], preferred_element_type=F32)
    ar = ap_ref[0, 0:1, :]
    ai = ap_ref[0, 1:2, :]

    def step(k, carry):
        h_re, h_im = carry
        c = (n_chunks - 1 - k) if reverse else k
        r = pl.ds(pl.multiple_of(c * rows, rows), rows)
        p_sc[r, 0:n] = h_re
        p_sc[r, n:2 * n] = h_im
        return (ar * h_re - ai * h_im + s_sc[r, 0:n], ar * h_im + ai * h_re + s_sc[r, n:2 * n])

    h_re, h_im = lax.fori_loop(0, n_chunks, step, (h_re_sc[...], h_im_sc[...]))
    h_re_sc[...] = h_re
    h_im_sc[...] = h_im

    y = jnp.dot(p_sc[...].astype(BF16), wout_ref[0], preferred_element_type=F32)
    if with_toeplitz:
        y_ref[0] = y + jnp.dot(x, m_ref[0], preferred_element_type=F32)
    else:
        y_ref[0] = _gelu_tanh(y + part_ref[0]).astype(y_ref.dtype)


def _s5_chunked(x, m8, fwd_ops, bwd_ops, blk0, *, rows, tr=512):
    O, R, W = x.shape
    n2 = fwd_ops[0].shape[2]
    tr = min(tr, R)
    nt = R // tr
    scratch = [pltpu.VMEM((rows, n2 // 2), F32), pltpu.VMEM((rows, n2 // 2), F32),
               pltpu.VMEM((tr, n2), F32), pltpu.VMEM((tr, n2), F32)]

    def row_block(order):
        return pl.BlockSpec((1, tr, W), lambda o, r: (o, order(r), 0))

    def per_block(shape):
        return pl.BlockSpec((1,) + shape, lambda o, r: (blk0 + o, 0, 0))

    wst, wout, apow = fwd_ops
    part = pl.pallas_call(
        functools.partial(_s5_pass_kernel, rows=rows, n_chunks=tr // rows, reverse=False,
                          with_toeplitz=True),
        grid=(O, nt),
        in_specs=[row_block(lambda r: r), per_block((W, W)), per_block((W, n2)),
                  per_block((n2, W)), per_block((SUBLANES, n2 // 2))],
        out_specs=row_block(lambda r: r),
        out_shape=jax.ShapeDtypeStruct((O, R, W), F32),
        scratch_shapes=scratch,
        compiler_params=_cparams(("parallel", "arbitrary")),
        name="s5_forward",
    )(x, m8, wst, wout, apow)
    wst, wout, apow = bwd_ops
    return pl.pallas_call(
        functools.partial(_s5_pass_kernel, rows=rows, n_chunks=tr // rows, reverse=True,
                          with_toeplitz=False),
        grid=(O, nt),
        in_specs=[row_block(lambda r: nt - 1 - r), per_block((W, n2)), per_block((n2, W)),
                  per_block((SUBLANES, n2 // 2)), row_block(lambda r: nt - 1 - r)],
        out_specs=row_block(lambda r: nt - 1 - r),
        out_shape=jax.ShapeDtypeStruct((O, R, W), BF16),
        scratch_shapes=scratch,
        compiler_params=_cparams(("parallel", "arbitrary")),
        name="s5_backward",
    )(x, wst, wout, apow, part)


def _glu_merge_kernel(y_ref, o_ref, ga_ref, gb_ref, wg_ref, wa_ref, wb_ref, out_ref):
    y = y_ref[...]
    z = jnp.dot(y, wg_ref[...], preferred_element_type=F32)
    s = (y.astype(F32) * _sigmoid(z)).astype(BF16)
    a_out = jnp.dot(o_ref[...], wa_ref[...], preferred_element_type=F32)
    b_out = jnp.dot(s, wb_ref[...], preferred_element_type=F32)
    merged = (_sigmoid(ga_ref[...].astype(F32)) * a_out + _sigmoid(gb_ref[...].astype(F32)) * b_out)
    out_ref[...] = merged.astype(out_ref.dtype)


def _glu_merge(y, o, h, wg, wa, wb, *, ga_blk, gb_blk, tm=256):
    T, SW = y.shape
    AW = o.shape[1]
    D = wa.shape[1]
    tm = min(tm, T)
    return pl.pallas_call(
        _glu_merge_kernel,
        grid=(T // tm,),
        in_specs=[pl.BlockSpec((tm, SW), lambda i: (i, 0)),
                  pl.BlockSpec((tm, AW), lambda i: (i, 0)),
                  pl.BlockSpec((tm, D), lambda i: (i, ga_blk)),
                  pl.BlockSpec((tm, D), lambda i: (i, gb_blk)),
                  pl.BlockSpec((SW, SW), lambda i: (0, 0)),
                  pl.BlockSpec((AW, D), lambda i: (0, 0)),
                  pl.BlockSpec((SW, D), lambda i: (0, 0))],
        out_specs=pl.BlockSpec((tm, D), lambda i: (i, 0)),
        out_shape=jax.ShapeDtypeStruct((T, D), BF16),
        compiler_params=_cparams(("parallel",)),
        name="glu_merge",
    )(y, o, h, h, wg, wa, wb)


def _outproj_kernel(m_ref, x_ref, w_ref, o_ref):
    o_ref[...] = x_ref[...] + jnp.dot(m_ref[...], w_ref[...], preferred_element_type=F32)


def _outproj(merged, x2d, w, *, tm=512):
    T, D = x2d.shape
    tm = min(tm, T)
    return pl.pallas_call(
        _outproj_kernel,
        grid=(T // tm,),
        in_specs=[pl.BlockSpec((tm, D), lambda i: (i, 0)),
                  pl.BlockSpec((tm, D), lambda i: (i, 0)),
                  pl.BlockSpec((D, D), lambda i: (0, 0))],
        out_specs=pl.BlockSpec((tm, D), lambda i: (i, 0)),
        out_shape=jax.ShapeDtypeStruct((T, D), F32),
        compiler_params=_cparams(("parallel",)),
        name="outproj",
    )(merged, x2d, w)


def _router_kernel(x_ref, g_ref, whi_ref, wlo_ref, b_ref, xe_ref, meta_ref, cnt_ref,
                   tri_sc, eye_sc, carry_sc, *, n_groups, per_group, e_row0):
    t = pl.program_id(0)
    tm, D = x_ref.shape
    n_rows = whi_ref.shape[0]

    @pl.when(t == 0)
    def _():
        r = lax.broadcasted_iota(jnp.int32, (tm, tm), 0)
        c = lax.broadcasted_iota(jnp.int32, (tm, tm), 1)
        tri_sc[...] = (r < c).astype(BF16)
        eye_sc[...] = (r == c).astype(BF16)
        carry_sc[...] = jnp.zeros(carry_sc.shape, F32)

    x = x_ref[...]
    ms = jnp.mean(x * x, axis=-1, keepdims=True)
    xn = x * lax.rsqrt(ms + RMS_EPS) * g_ref[...]
    xe_ref[:, 0:D] = xn

    nt = (((1,), (1,)), ((), ()))
    x_hi, x_lo = _split_bf16(xn)
    whi, wlo = whi_ref[...], wlo_ref[...]
    logits = (lax.dot_general(whi, x_hi, nt, preferred_element_type=F32)
              + lax.dot_general(whi, x_lo, nt, preferred_element_type=F32)
              + lax.dot_general(wlo, x_hi, nt, preferred_element_type=F32)) + b_ref[...]

    g = [logits[k:k + 1, :] for k in range(n_groups)]
    g_max = functools.reduce(jnp.maximum, g)
    g_idx = jnp.full(g_max.shape, n_groups - 1, jnp.int32)
    for k in range(n_groups - 2, -1, -1):
        g_idx = jnp.where(g[k] == g_max, k, g_idx)
    g_w = 1.0 / functools.reduce(lambda a, b: a + b, [jnp.exp(v - g_max) for v in g])

    e_in = []
    for j in range(per_group):
        v = logits[e_row0 + j:e_row0 + j + 1, :]
        for k in range(1, n_groups):
            row = e_row0 + k * per_group + j
            v = jnp.where(g_idx == k, logits[row:row + 1, :], v)
        e_in.append(v)

    def first_argmax(vals):
        vmax = functools.reduce(jnp.maximum, vals)
        idx = jnp.full(vmax.shape, len(vals) - 1, jnp.int32)
        for k in range(len(vals) - 2, -1, -1):
            idx = jnp.where(vals[k] == vmax, k, idx)
        return vmax, idx

    v1, i1 = first_argmax(e_in)
    rest = [jnp.where(i1 == j, -jnp.inf, e_in[j]) for j in range(per_group)]
    v2, i2 = first_argmax(rest)
    e2 = jnp.exp(v2 - v1)
    w1 = g_w / (1.0 + e2)
    w2 = g_w * e2 / (1.0 + e2)
    lo_first = i1 < i2
    a = jnp.where(lo_first, i1, i2)
    b = jnp.where(lo_first, i2, i1)
    w_a = jnp.where(lo_first, w1, w2)
    w_b = jnp.where(lo_first, w2, w1)
    pair = jnp.where(a == 0, b - 1, jnp.where(a == 1, b + 1, 5))
    bucket = g_idx * N_PAIRS + pair

    rows = lax.broadcasted_iota(jnp.int32, (n_rows, tm), 0)
    onehot = (rows == bucket).astype(F32)
    prefix = jnp.dot(onehot.astype(BF16), tri_sc[...], preferred_element_type=F32)
    rank = jnp.sum(onehot * (prefix + carry_sc[...]), axis=0, keepdims=True)
    carry_sc[...] = carry_sc[...] + jnp.sum(onehot, axis=1, keepdims=True)
    cnt_ref[...] = jnp.broadcast_to(carry_sc[...], cnt_ref.shape)

    meta_ref[0, 0:1, :] = bucket.astype(F32)
    meta_ref[0, 1:2, :] = rank
    meta_ref[0, 2:3, :] = w_a
    meta_ref[0, 3:4, :] = w_b
    meta_ref[0, 4:8, :] = jnp.zeros((4, tm), F32)

    eye = eye_sc[...]
    for blk, w in enumerate((w_a, w_b)):
        wrep = jnp.broadcast_to(w, (LANES, tm))
        hi, lo = _split_bf16(wrep)
        col = (lax.dot_general(eye, hi, nt, preferred_element_type=F32)
               + lax.dot_general(eye, lo, nt, preferred_element_type=F32))
        xe_ref[:, D + blk * LANES:D + (blk + 1) * LANES] = col


def _router(x2d, gain, w_rg, b_rg, w_re, b_re, *, tm=512):
    T, D = x2d.shape
    n_groups = w_rg.shape[1]
    n_exp = w_re.shape[1]
    per_group = n_exp // n_groups
    e_row0 = SUBLANES
    n_rows = e_row0 + n_exp
    n_rows = ((n_rows + SUBLANES - 1) // SUBLANES) * SUBLANES
    assert n_groups * N_PAIRS <= n_rows and per_group == 4
    wt = jnp.zeros((n_rows, D), F32)
    wt = wt.at[0:n_groups].set(w_rg.astype(F32).T).at[e_row0:e_row0 + n_exp].set(w_re.astype(F32).T)
    bias = jnp.zeros((n_rows, 1), F32)
    bias = bias.at[0:n_groups, 0].set(b_rg.astype(F32)).at[e_row0:e_row0 + n_exp, 0].set(b_re.astype(F32))
    w_hi, w_lo = _split_bf16(wt)
    tm = min(tm, T)
    De = D + 2 * LANES
    return pl.pallas_call(
        functools.partial(_router_kernel, n_groups=n_groups, per_group=per_group, e_row0=e_row0),
        grid=(T // tm,),
        in_specs=[pl.BlockSpec((tm, D), lambda t: (t, 0)),
                  pl.BlockSpec((1, D), lambda t: (0, 0)),
                  pl.BlockSpec((n_rows, D), lambda t: (0, 0)),
                  pl.BlockSpec((n_rows, D), lambda t: (0, 0)),
                  pl.BlockSpec((n_rows, 1), lambda t: (0, 0))],
        out_specs=[pl.BlockSpec((tm, De), lambda t: (t, 0)),
                   pl.BlockSpec((1, SUBLANES, tm), lambda t: (t, 0, 0)),
                   pl.BlockSpec((n_rows, LANES), lambda t: (0, 0))],
        out_shape=[jax.ShapeDtypeStruct((T, De), F32),
                   jax.ShapeDtypeStruct((T // tm, SUBLANES, tm), F32),
                   jax.ShapeDtypeStruct((n_rows, LANES), F32)],
        scratch_shapes=[pltpu.VMEM((tm, tm), BF16), pltpu.VMEM((tm, tm), BF16),
                        pltpu.VMEM((n_rows, 1), F32)],
        compiler_params=_cparams(("arbitrary",)),
        name="router",
    )(x2d, gain.reshape(1, D).astype(F32), w_hi, w_lo, bias)


def _dispatch_kernel(pos_ref, x_ref, xs_in_ref, xs_ref, sem, *, tm):
    del xs_in_ref
    base = pl.program_id(0) * tm

    def issue(r2, carry):
        for lane in range(2):
            r = 2 * r2 + lane
            pltpu.make_async_copy(x_ref.at[pl.ds(r, 1)], xs_ref.at[pl.ds(pos_ref[base + r], 1)],
                                  sem).start(priority=lane)
        return carry

    lax.fori_loop(0, tm // 2, issue, 0, unroll=4)

    def drain(r, carry):
        pltpu.make_async_copy(x_ref.at[pl.ds(0, 1)], xs_ref.at[pl.ds(0, 1)], sem).wait()
        return carry

    lax.fori_loop(0, tm, drain, 0, unroll=8)


def _dispatch(pos, xe, n_rows, *, tm=512):
    T, De = xe.shape
    tm = min(tm, T)
    xs0 = jnp.zeros((n_rows, De), F32)
    return pl.pallas_call(
        functools.partial(_dispatch_kernel, tm=tm),
        grid_spec=pltpu.PrefetchScalarGridSpec(
            num_scalar_prefetch=1,
            grid=(T // tm,),
            in_specs=[pl.BlockSpec((tm, De), lambda t, pos: (t, 0)),
                      pl.BlockSpec(memory_space=pl.ANY)],
            out_specs=pl.BlockSpec(memory_space=pl.ANY),
            scratch_shapes=[pltpu.SemaphoreType.DMA(())]),
        out_shape=jax.ShapeDtypeStruct((n_rows, De), F32),
        input_output_aliases={2: 0},
        compiler_params=_cparams(("arbitrary",)),
        name="moe_dispatch",
    )(pos, xe, xs0)


def _collect_kernel(pos_ref, ys_ref, x_ref, o_ref, buf, sem, *, tm):
    base = pl.program_id(0) * tm

    def issue(r2, carry):
        for lane in range(2):
            r = 2 * r2 + lane
            pltpu.make_async_copy(ys_ref.at[pl.ds(pos_ref[base + r], 1)], buf.at[pl.ds(r, 1)],
                                  sem).start(priority=lane)
        return carry

    lax.fori_loop(0, tm // 2, issue, 0, unroll=4)

    def drain(r, carry):
        pltpu.make_async_copy(ys_ref.at[pl.ds(0, 1)], buf.at[pl.ds(0, 1)], sem).wait()
        return carry

    lax.fori_loop(0, tm, drain, 0, unroll=8)
    o_ref[...] = x_ref[...] + buf[...]


def _collect(pos, ys, x2d, *, tm=512):
    T, D = x2d.shape
    tm = min(tm, T)
    return pl.pallas_call(
        functools.partial(_collect_kernel, tm=tm),
        grid_spec=pltpu.PrefetchScalarGridSpec(
            num_scalar_prefetch=1,
            grid=(T // tm,),
            in_specs=[pl.BlockSpec(memory_space=pl.ANY),
                      pl.BlockSpec((tm, D), lambda t, pos: (t, 0))],
            out_specs=pl.BlockSpec((tm, D), lambda t, pos: (t, 0)),
            scratch_shapes=[pltpu.VMEM((tm, D), F32), pltpu.SemaphoreType.DMA(())]),
        out_shape=jax.ShapeDtypeStruct((T, D), F32),
        compiler_params=_cparams(("arbitrary",)),
        name="moe_collect",
    )(pos, ys, x2d)


def _ffn_kernel(te_ref, tv_ref, x_ref, w1_ref, w3_ref, w2_ref, o_ref, xb_sc, *, D):
    i, j, f = pl.program_id(0), pl.program_id(1), pl.program_id(2)
    first = jnp.logical_and(j == 0, f == 0)

    @pl.when(first)
    def _():
        xb_sc[...] = x_ref[:, 0:D].astype(BF16)
        o_ref[...] = jnp.zeros(o_ref.shape, F32)

    @pl.when(tv_ref[i] != 0)
    def _():
        xb = xb_sc[...]
        h1 = jnp.dot(xb, w1_ref[0], preferred_element_type=F32)
        h3 = jnp.dot(xb, w3_ref[0], preferred_element_type=F32)
        hid = (h1 * _sigmoid(h1) * h3).astype(BF16)
        y = jnp.dot(hid, w2_ref[0], preferred_element_type=F32)
        w = jnp.where(j == 0, x_ref[:, D:D + 1], x_ref[:, D + LANES:D + LANES + 1])
        o_ref[...] += w * y


def _expert_ffn(tile_expert, tile_valid, xs, w1, w3, w2, *, tm, tf=512):
    R, De = xs.shape
    E, D, FF = w1.shape
    tf = min(tf, FF)
    n_tiles = R // tm
    return pl.pallas_call(
        functools.partial(_ffn_kernel, D=D),
        grid_spec=pltpu.PrefetchScalarGridSpec(
            num_scalar_prefetch=2,
            grid=(n_tiles, 2, FF // tf),
            in_specs=[pl.BlockSpec((tm, De), lambda i, j, f, te, tv: (i, 0)),
                      pl.BlockSpec((1, D, tf), lambda i, j, f, te, tv: (te[2 * i + j], 0, f)),
                      pl.BlockSpec((1, D, tf), lambda i, j, f, te, tv: (te[2 * i + j], 0, f)),
                      pl.BlockSpec((1, tf, D), lambda i, j, f, te, tv: (te[2 * i + j], f, 0))],
            out_specs=pl.BlockSpec((tm, D), lambda i, j, f, te, tv: (i, 0)),
            scratch_shapes=[pltpu.VMEM((tm, D), BF16)]),
        out_shape=jax.ShapeDtypeStruct((R, D), F32),
        compiler_params=_cparams(("parallel", "arbitrary", "arbitrary")),
        name="expert_ffn",
    )(tile_expert, tile_valid, xs, w1, w3, w2)


def _moe_plan(meta, counts, n_groups, per_group, T, tm_e):
    n_buckets = n_groups * N_PAIRS
    bucket = meta[:, 0, :].reshape(T).astype(jnp.int32)
    rank = meta[:, 1, :].reshape(T).astype(jnp.int32)
    cnt = counts[:n_buckets, 0].astype(jnp.int32)
    padded = ((cnt + tm_e - 1) // tm_e) * tm_e
    ends = jnp.cumsum(padded)
    offs = ends - padded
    pos = offs[bucket] + rank
    n_tiles = T // tm_e + n_buckets
    starts = jnp.arange(n_tiles, dtype=jnp.int32) * tm_e
    tile_valid = (starts < ends[-1]).astype(jnp.int32)
    last_start = jnp.maximum(ends[-1] - tm_e, 0)
    tb = jnp.sum((ends[None, :] <= jnp.minimum(starts, last_start)[:, None]).astype(jnp.int32), axis=1)
    tb = jnp.minimum(tb, n_buckets - 1)
    grp, pair = tb // N_PAIRS, tb % N_PAIRS
    ea = grp * per_group + jnp.asarray(_PAIR_A, jnp.int32)[pair]
    eb = grp * per_group + jnp.asarray(_PAIR_B, jnp.int32)[pair]
    tile_expert = jnp.stack([ea, eb], axis=1).reshape(-1)
    return pos, tile_expert, tile_valid, n_tiles


def _hier_moe_residual(x2d, gain, w_rg, b_rg, w_re, b_re, w1, w3, w2, expert0, *, tm_e=512):
    T, D = x2d.shape
    n_groups = w_rg.shape[1]
    per_group = w_re.shape[1] // n_groups
    tm_e = min(tm_e, T)
    xe, meta, counts = _router(x2d, gain, w_rg, b_rg, w_re, b_re)
    pos, tile_expert, tile_valid, n_tiles = _moe_plan(meta, counts, n_groups, per_group, T, tm_e)
    xs = _dispatch(pos, xe, n_tiles * tm_e)
    ys = _expert_ffn(tile_expert + expert0, tile_valid, xs, w1, w3, w2, tm=tm_e)
    return _collect(pos, ys, x2d)


def _hybrid_mixer_residual(x2d, B, S, layer_idx, slopes, norm1, w_in, q_gain, k_gain, lam_q1, lam_k1,
                           lam_q2, lam_k2, head_gain, w_attn_up, s5_ops, w_glu, w_ssm_up, w_out):
    T, D = x2d.shape
    dk = q_gain.shape[0]
    dv = head_gain.shape[0]
    AW, SW = w_attn_up.shape[0], w_glu.shape[0]
    heads = AW // dv
    assert dv == 2 * dk == LANES
    u_off = 3 * AW
    ga_off = u_off + SW
    gb_off = ga_off + D
    assert ga_off % D == 0

    h = _norm_inproj(x2d, norm1, w_in.astype(BF16))

    lam_init = 0.8 - 0.6 * math.exp(-0.3 * layer_idx)
    lam = (jnp.exp(jnp.sum(lam_q1.astype(F32) * lam_k1.astype(F32)))
           - jnp.exp(jnp.sum(lam_q2.astype(F32) * lam_k2.astype(F32))) + lam_init)
    scal = jnp.stack([lam, jnp.asarray(1.0 - lam_init, F32)]).astype(F32)
    o = _diff_attention(h.reshape(B, S, -1), scal, slopes, q_gain, k_gain, head_gain,
                        heads=heads, dk=dk, k_blk=AW // dv, v_blk=2 * AW // dv)
    o = o.reshape(T, AW)

    L = min(S5_CHUNK, S)
    nc = S // L
    n_blk = SW // LANES
    assert B % SUBLANES == 0 and SW % LANES == 0
    m8, fwd_ops, bwd_ops = s5_ops
    u = h[:, u_off:u_off + SW].reshape(B, nc, L, n_blk, LANES)
    ut = u.transpose(3, 1, 0, 2, 4).reshape(n_blk, nc * B, L * LANES)
    yt = _s5_chunked(ut, m8, fwd_ops, bwd_ops, layer_idx * n_blk, rows=B)
    y = yt.reshape(n_blk, nc, B, L, LANES).transpose(2, 1, 3, 0, 4).reshape(T, SW)

    merged = _glu_merge(y, o, h, w_glu.astype(BF16), w_attn_up.astype(BF16), w_ssm_up.astype(BF16),
                        ga_blk=ga_off // D, gb_blk=gb_off // D)
    return _outproj(merged, x2d, w_out.astype(BF16))


def kernel(x, norm1, w_in, q_gain, k_gain, lam_q1, lam_k1, lam_q2, lam_k2, head_gain, w_attn_up, ssm_lam_re, ssm_lam_im, ssm_log_dt, ssm_b_re, ssm_b_im, ssm_c_re, ssm_c_im, ssm_d, w_glu, w_ssm_up, w_out, norm2, w_router_group, b_router_group, w_router_expert, b_router_expert, w1, w3, w2):
    B, S, D = x.shape
    depth = norm1.shape[0]
    heads = w_attn_up.shape[1] // head_gain.shape[1]
    slopes = jnp.exp2(-8.0 * jnp.arange(1, heads + 1, dtype=F32) / heads)
    x2d = x.reshape(B * S, D).astype(F32)
    n_exp = w1.shape[1]
    w1b, w3b, w2b = (w.astype(BF16).reshape((depth * n_exp,) + w.shape[2:]) for w in (w1, w3, w2))
    assert LANES % ssm_b_re.shape[4] == 0
    s5_all = jax.vmap(functools.partial(_s5_operators, L=min(S5_CHUNK, S)))(
        ssm_lam_re, ssm_lam_im, ssm_log_dt, ssm_b_re, ssm_b_im, ssm_c_re, ssm_c_im, ssm_d)
    s5_ops = jax.tree.map(lambda a: a.reshape((a.shape[0] * a.shape[1],) + a.shape[2:]), s5_all)
    for l in range(depth):
        x2d = _hybrid_mixer_residual(
            x2d, B, S, l, slopes, norm1[l], w_in[l], q_gain[l], k_gain[l], lam_q1[l], lam_k1[l],
            lam_q2[l], lam_k2[l], head_gain[l], w_attn_up[l], s5_ops, w_glu[l], w_ssm_up[l], w_out[l])
        x2d = _hier_moe_residual(
            x2d, norm2[l], w_router_group[l], b_router_group[l], w_router_expert[l],
            b_router_expert[l], w1b, w3b, w2b, l * n_exp)
    return x2d.reshape(B, S, D).astype(x.dtype)
```
